```python
import math
import jax, jax.numpy as jnp
from jax import lax
import numpy as np

D_MODEL = 1024
BATCH = 2
SEQ = 8192
DEPTH = 2
DEC_BATCH = 128
DEC_SEQ = 4
PAST_LEN = 16384
PAGE_SIZE = 128

N_META = 16
N_A_LAYERS = DEPTH // 2
N_B_LAYERS = DEPTH - N_A_LAYERS
CONV_A_WIDTH = 31
FFN_CONV_WIDTH = 3
D_FF = ((8 * D_MODEL // 3 + 127) // 128) * 128
N_HEADS = D_MODEL // 64
QK_NOPE = 64
QK_ROPE = 32
V_HEAD = 64
KV_LORA = D_MODEL // 8
Q_LORA = D_MODEL // 4
ROPE_THETA = 10000.0
NORM_EPS = 1e-6
Q_BLOCK = 128
SM_SCALE = 1.0 / math.sqrt(QK_NOPE + QK_ROPE)
NEG = -1e30

kernel_name = 'yoco_conformer_conv_mla_convffn_step'


def rms_norm(x, g):
    xf = x.astype(jnp.float32)
    y = xf * lax.rsqrt(jnp.mean(xf * xf, axis=-1, keepdims=True) + NORM_EPS)
    return (y * g.astype(jnp.float32)).astype(x.dtype)


def layer_norm(x, g, b):
    xf = x.astype(jnp.float32)
    mu = jnp.mean(xf, axis=-1, keepdims=True)
    var = jnp.mean(jnp.square(xf - mu), axis=-1, keepdims=True)
    y = (xf - mu) * lax.rsqrt(var + NORM_EPS) * g.astype(jnp.float32) + b.astype(jnp.float32)
    return y.astype(x.dtype)


def causal_dwconv(x, prefix, w):
    width = w.shape[0]
    xp = jnp.concatenate([prefix.astype(x.dtype), x], axis=1)
    y = lax.conv_general_dilated(xp, w[:, None, :].astype(x.dtype), window_strides=(1,), padding='VALID',
                                 dimension_numbers=('NWC', 'WIO', 'NWC'), feature_group_count=x.shape[-1])
    return y, xp[:, xp.shape[1] - (width - 1):]


def rope(x, pos):
    half = x.shape[-1] // 2
    inv = ROPE_THETA ** (-jnp.arange(half, dtype=jnp.float32) / half)
    ang = pos.astype(jnp.float32)[:, None] * inv[None, :]
    shape = (1, pos.shape[0]) + (1,) * (x.ndim - 3) + (half,)
    cos = jnp.cos(ang).reshape(shape)
    sin = jnp.sin(ang).reshape(shape)
    xf = x.astype(jnp.float32)
    x1, x2 = xf[..., :half], xf[..., half:]
    return jnp.concatenate([x1 * cos - x2 * sin, x1 * sin + x2 * cos], axis=-1).astype(x.dtype)


def conv_module(x, prefix, w_pw1, b_pw1, w_dw, b_dw, ln_g, ln_b, w_pw2, b_pw2):
    a, g = jnp.split(x @ w_pw1 + b_pw1, 2, axis=-1)
    u = a * jax.nn.sigmoid(g)
    y, tail = causal_dwconv(u, prefix, w_dw)
    y = jax.nn.silu(layer_norm(y + b_dw, ln_g, ln_b))
    return y @ w_pw2 + b_pw2, tail


def conv_ffn(x, prefix, w_up, w_dw, w_down):
    h, tail = causal_dwconv(x @ w_up, prefix, w_dw)
    g, u = jnp.split(h, 2, axis=-1)
    return (jax.nn.silu(g) * u) @ w_down, tail


def mla_shared_kv(h, pos, kv_norm, w_dkv, lat_norm, w_kr, knorm_rope):
    hn = rms_norm(h, kv_norm)
    c = rms_norm(hn @ w_dkv, lat_norm)
    kr = rope(rms_norm(hn @ w_kr, knorm_rope), pos)
    return c, kr


def mla_queries(xn, pos, w_dq, q_lat_norm, w_uq, qnorm_nope, qnorm_rope):
    q = jnp.einsum('btl,lhd->bthd', rms_norm(xn @ w_dq, q_lat_norm), w_uq)
    q_nope = rms_norm(q[..., :QK_NOPE], qnorm_nope)
    q_rope = rope(rms_norm(q[..., QK_NOPE:], qnorm_rope), pos)
    return q_nope, q_rope


def key_nope(c, w_uk, knorm_nope):
    return rms_norm(jnp.einsum('btc,chd->bthd', c, w_uk), knorm_nope)


def mla_attend_prompt(q_nope, q_rope, c, kr, w_uk, w_uv, knorm_nope):
    b, t = q_nope.shape[0], q_nope.shape[1]
    k_nope = key_nope(c, w_uk, knorm_nope)
    v = jnp.einsum('btc,chd->bthd', c, w_uv)
    n_blk = -(-t // Q_BLOCK)
    pad = n_blk * Q_BLOCK - t

    def blocks(q):
        q = jnp.pad(q, ((0, 0), (0, pad), (0, 0), (0, 0)))
        return q.reshape(b, n_blk, Q_BLOCK, q.shape[2], q.shape[3]).transpose(1, 0, 2, 3, 4)

    kpos = jnp.arange(t)

    def block(args):
        qn_b, qr_b, start = args
        s = (jnp.einsum('bqhd,bkhd->bhqk', qn_b, k_nope) +
             jnp.einsum('bqhr,bkr->bhqk', qr_b, kr)).astype(jnp.float32) * SM_SCALE
        qpos = start + jnp.arange(Q_BLOCK)
        s = jnp.where(kpos[None, :] <= qpos[:, None], s, NEG)
        p = jax.nn.softmax(s, axis=-1).astype(v.dtype)
        return jnp.einsum('bhqk,bkhd->bqhd', p, v)

    o = lax.map(block, (blocks(q_nope), blocks(q_rope), jnp.arange(n_blk) * Q_BLOCK))
    o = o.transpose(1, 0, 2, 3, 4).reshape(b, n_blk * Q_BLOCK, N_HEADS * V_HEAD)
    return o[:, :t]


def mla_attend_sample(q_nope, q_rope, c_new, kr_new, cache_lat, cache_kr, page_table, w_uk, w_uv, knorm_nope):
    db, q = q_nope.shape[0], q_nope.shape[1]
    k_new = key_nope(c_new, w_uk, knorm_nope)
    s = (jnp.einsum('bqhd,bkhd->bhqk', q_nope, k_new) +
         jnp.einsum('bqhr,bkr->bhqk', q_rope, kr_new)).astype(jnp.float32) * SM_SCALE
    causal = jnp.arange(q)[None, :] <= jnp.arange(q)[:, None]
    s = jnp.where(causal, s, NEG)
    m = jnp.max(s, axis=-1)
    p = jnp.exp(s - m[..., None])
    l = jnp.sum(p, axis=-1)
    acc = jnp.einsum('bhqk,bkc->bhqc', p, c_new.astype(jnp.float32))

    def step(carry, phys):
        m, l, acc = carry
        c = cache_lat[phys]
        kr = cache_kr[phys]
        k = key_nope(c, w_uk, knorm_nope)
        s = (jnp.einsum('bqhd,bphd->bhqp', q_nope, k) +
             jnp.einsum('bqhr,bpr->bhqp', q_rope, kr)).astype(jnp.float32) * SM_SCALE
        m_new = jnp.maximum(m, jnp.max(s, axis=-1))
        alpha = jnp.exp(m - m_new)
        p = jnp.exp(s - m_new[..., None])
        l = l * alpha + jnp.sum(p, axis=-1)
        acc = acc * alpha[..., None] + jnp.einsum('bhqp,bpc->bhqc', p, c.astype(jnp.float32))
        return (m_new, l, acc), None

    (m, l, acc), _ = lax.scan(step, (m, l, acc), page_table.T)
    o_lat = acc / l[..., None]
    o = jnp.einsum('bhqc,chv->bqhv', o_lat, w_uv.astype(jnp.float32)).astype(q_nope.dtype)
    return o.reshape(db, q, N_HEADS * V_HEAD)


def setup_inputs(seed: int = 0) -> dict:
    key = jax.random.key(seed)
    ks = iter(jax.random.split(key, 48))

    def nrm(shape, scale):
        return jax.random.normal(next(ks), shape, jnp.float32) * scale

    def gain(shape):
        return 1.0 + nrm(shape, 0.05)

    n_pages = PAST_LEN // PAGE_SIZE
    n_used = DEC_BATCH * n_pages
    n_phys = n_used + max(1, n_used // 4)
    page_table = jax.random.permutation(next(ks), n_phys)[:n_used].reshape(DEC_BATCH, n_pages).astype(jnp.int32)
    f2 = 2 * D_FF
    return {
        'x_prompt': nrm((BATCH, SEQ, D_MODEL), 1.0),
        'x_sample': nrm((DEC_BATCH, DEC_SEQ, D_MODEL), 1.0),
        'state_conv_a': nrm((N_A_LAYERS, DEC_BATCH, CONV_A_WIDTH - 1, D_MODEL), 0.5),
        'state_ffn_conv': nrm((DEPTH, DEC_BATCH, FFN_CONV_WIDTH - 1, f2), 1.0),
        'cache_kv_latent': nrm((n_phys, PAGE_SIZE, KV_LORA), 1.0),
        'cache_k_rope': nrm((n_phys, PAGE_SIZE, QK_ROPE), 1.0),
        'page_table': page_table,
        'meta_tokens': nrm((N_META, D_MODEL), 1.0),
        'norm_mix': gain((DEPTH, D_MODEL)),
        'norm_ffn': gain((DEPTH, D_MODEL)),
        'a_w_pw1': nrm((N_A_LAYERS, D_MODEL, 2 * D_MODEL), D_MODEL ** -0.5),
        'a_b_pw1': nrm((N_A_LAYERS, 2 * D_MODEL), 0.02),
        'a_w_dw': nrm((N_A_LAYERS, CONV_A_WIDTH, D_MODEL), CONV_A_WIDTH ** -0.5),
        'a_b_dw': nrm((N_A_LAYERS, D_MODEL), 0.02),
        'a_ln_g': gain((N_A_LAYERS, D_MODEL)),
        'a_ln_b': nrm((N_A_LAYERS, D_MODEL), 0.02),
        'a_w_pw2': nrm((N_A_LAYERS, D_MODEL, D_MODEL), D_MODEL ** -0.5),
        'a_b_pw2': nrm((N_A_LAYERS, D_MODEL), 0.02),
        'ffn_w_up': nrm((DEPTH, D_MODEL, f2), D_MODEL ** -0.5),
        'ffn_w_dw': nrm((DEPTH, FFN_CONV_WIDTH, f2), FFN_CONV_WIDTH ** -0.5),
        'ffn_w_down': nrm((DEPTH, D_FF, D_MODEL), D_FF ** -0.5),
        'kv_norm': gain((D_MODEL,)),
        'mla_w_dkv': nrm((D_MODEL, KV_LORA), D_MODEL ** -0.5),
        'mla_lat_norm': gain((KV_LORA,)),
        'mla_w_kr': nrm((D_MODEL, QK_ROPE), D_MODEL ** -0.5),
        'mla_knorm_rope': gain((QK_ROPE,)),
        'mla_w_uk': nrm((KV_LORA, N_HEADS, QK_NOPE), KV_LORA ** -0.5),
        'mla_w_uv': nrm((KV_LORA, N_HEADS, V_HEAD), KV_LORA ** -0.5),
        'mla_knorm_nope': gain((QK_NOPE,)),
        'mla_w_dq': nrm((N_B_LAYERS, D_MODEL, Q_LORA), D_MODEL ** -0.5),
        'mla_q_lat_norm': gain((N_B_LAYERS, Q_LORA)),
        'mla_w_uq': nrm((N_B_LAYERS, Q_LORA, N_HEADS, QK_NOPE + QK_ROPE), Q_LORA ** -0.5),
        'mla_qnorm_nope': gain((N_B_LAYERS, QK_NOPE)),
        'mla_qnorm_rope': gain((N_B_LAYERS, QK_ROPE)),
        'mla_w_o': nrm((N_B_LAYERS, N_HEADS * V_HEAD, D_MODEL), (N_HEADS * V_HEAD) ** -0.5),
    }


def reference(x_prompt, x_sample, state_conv_a, state_ffn_conv, cache_kv_latent, cache_k_rope, page_table,
              meta_tokens, norm_mix, norm_ffn,
              a_w_pw1, a_b_pw1, a_w_dw, a_b_dw, a_ln_g, a_ln_b, a_w_pw2, a_b_pw2,
              ffn_w_up, ffn_w_dw, ffn_w_down,
              kv_norm, mla_w_dkv, mla_lat_norm, mla_w_kr, mla_knorm_rope, mla_w_uk, mla_w_uv, mla_knorm_nope,
              mla_w_dq, mla_q_lat_norm, mla_w_uq, mla_qnorm_nope, mla_qnorm_rope, mla_w_o):
    dt = x_prompt.dtype
    b_p = x_prompt.shape[0]
    meta = jnp.broadcast_to(meta_tokens.astype(dt)[None], (b_p, N_META, D_MODEL))
    hp = jnp.concatenate([meta, x_prompt], axis=1)
    hs = x_sample
    pos_p = jnp.arange(hp.shape[1], dtype=jnp.int32)
    pos_s = PAST_LEN + jnp.arange(hs.shape[1], dtype=jnp.int32)
    zero_conv_a = jnp.zeros((b_p, CONV_A_WIDTH - 1, D_MODEL), dt)
    zero_ffn = jnp.zeros((b_p, FFN_CONV_WIDTH - 1, 2 * D_FF), dt)
    conv_a_p, conv_a_s, ffn_p, ffn_s = [], [], [], []
    for layer in range(DEPTH):
        if layer < N_A_LAYERS:
            a_params = (a_w_pw1[layer], a_b_pw1[layer], a_w_dw[layer], a_b_dw[layer],
                        a_ln_g[layer], a_ln_b[layer], a_w_pw2[layer], a_b_pw2[layer])
            d_p, t_p = conv_module(rms_norm(hp, norm_mix[layer]), zero_conv_a, *a_params)
            d_s, t_s = conv_module(rms_norm(hs, norm_mix[layer]), state_conv_a[layer], *a_params)
            hp, hs = hp + d_p, hs + d_s
            conv_a_p.append(t_p)
            conv_a_s.append(t_s)
        else:
            if layer == N_A_LAYERS:
                kv_params = (kv_norm, mla_w_dkv, mla_lat_norm, mla_w_kr, mla_knorm_rope)
                c_p, kr_p = mla_shared_kv(hp, pos_p, *kv_params)
                c_s, kr_s = mla_shared_kv(hs, pos_s, *kv_params)
            j = layer - N_A_LAYERS
            q_params = (mla_w_dq[j], mla_q_lat_norm[j], mla_w_uq[j], mla_qnorm_nope[j], mla_qnorm_rope[j])
            qn_p, qr_p = mla_queries(rms_norm(hp, norm_mix[layer]), pos_p, *q_params)
            qn_s, qr_s = mla_queries(rms_norm(hs, norm_mix[layer]), pos_s, *q_params)
            o_p = mla_attend_prompt(qn_p, qr_p, c_p, kr_p, mla_w_uk, mla_w_uv, mla_knorm_nope)
            o_s = mla_attend_sample(qn_s, qr_s, c_s, kr_s, cache_kv_latent, cache_k_rope, page_table,
                                    mla_w_uk, mla_w_uv, mla_knorm_nope)
            hp = hp + o_p @ mla_w_o[j]
            hs = hs + o_s @ mla_w_o[j]
        f_params = (ffn_w_up[layer], ffn_w_dw[layer], ffn_w_down[layer])
        d_p, t_p = conv_ffn(rms_norm(hp, norm_ffn[layer]), zero_ffn, *f_params)
        d_s, t_s = conv_ffn(rms_norm(hs, norm_ffn[layer]), state_ffn_conv[layer], *f_params)
        hp, hs = hp + d_p, hs + d_s
        ffn_p.append(t_p)
        ffn_s.append(t_s)
    return (hp[:, N_META:], hs, jnp.stack(conv_a_p), jnp.stack(conv_a_s), jnp.stack(ffn_p), jnp.stack(ffn_s),
            c_p, kr_p, c_s, kr_s)
```

```python
import functools
import math

import jax
import jax.numpy as jnp
from jax import lax
from jax.experimental import pallas as pl
from jax.experimental.pallas import tpu as pltpu

F32 = jnp.float32
BF16 = jnp.bfloat16
NORM_EPS = 1e-6
ROPE_THETA = 10000.0
NEG = -1e30

LANES = 128
HEAD_SLOT = 128
ROW_TILE = 512
ATTN_Q_TILE = 512
ATTN_K_TILE = 512
ATTN_DIAG_TILE = 256
FFN_CHUNK = 256
PAGES_PER_STEP = 8
VMEM_LIMIT = 56 * 1024 * 1024


def _cparams(sem):
    return pltpu.CompilerParams(dimension_semantics=sem, vmem_limit_bytes=VMEM_LIMIT)


def _const(shape):
    zeros = (0,) * len(shape)
    return pl.BlockSpec(shape, lambda *_: zeros)


def _sigmoid(x):
    return 1.0 / (1.0 + jnp.exp(-x))


def _rms_scale(x):
    return lax.rsqrt(jnp.mean(x * x, axis=-1, keepdims=True) + NORM_EPS)


def _dot(a, b):
    return jnp.dot(a, b, preferred_element_type=F32)


def _dot_nt(a, b):
    return lax.dot_general(a, b, (((1,), (1,)), ((), ())), preferred_element_type=F32)


def _a1_kernel(x_ref, g_ref, w_ref, b_ref, u_ref):
    x = x_ref[...]
    xn = x * _rms_scale(x) * g_ref[...]
    z = _dot(xn.astype(BF16), w_ref[...]) + b_ref[...]
    d = u_ref.shape[-1]
    u_ref[...] = z[:, :d] * _sigmoid(z[:, d:])


def _a1_call(x, g, w, b, tm):
    bsz, t, d = x.shape
    return pl.pallas_call(
        _a1_kernel,
        grid=(bsz, t // tm),
        in_specs=[pl.BlockSpec((None, tm, d), lambda bi, i: (bi, i, 0)),
                  _const((1, d)), _const((d, 2 * d)), _const((1, 2 * d))],
        out_specs=pl.BlockSpec((None, tm, d), lambda bi, i: (bi, i, 0)),
        out_shape=jax.ShapeDtypeStruct((bsz, t, d), F32),
        compiler_params=_cparams(("parallel", "parallel")),
        name="conv_pw1_glu",
    )(x, g, w, b)


def _ln_swish_pw2(y, h, lng_ref, lnb_ref, w2_ref, b2_ref):
    mu = jnp.mean(y, axis=-1, keepdims=True)
    yc = y - mu
    var = jnp.mean(yc * yc, axis=-1, keepdims=True)
    yn = yc * lax.rsqrt(var + NORM_EPS) * lng_ref[...] + lnb_ref[...]
    a = yn * _sigmoid(yn)
    return h + _dot(a.astype(BF16), w2_ref[...]) + b2_ref[...]


def _a2_seq_kernel(*refs, tm, halo, width, has_prev):
    if has_prev:
        (u_ref, uh_ref, prev_ref, h_ref, wdw_ref, bdw_ref, lng_ref, lnb_ref, w2_ref, b2_ref,
         o_ref, ext_ref, y_ref) = refs
        i = pl.program_id(1)

        @pl.when(i == 0)
        def _():
            ext_ref[0:halo, :] = prev_ref[...]

        @pl.when(i > 0)
        def _():
            ext_ref[0:halo, :] = uh_ref[...]
    else:
        (u_ref, h_ref, wdw_ref, bdw_ref, lng_ref, lnb_ref, w2_ref, b2_ref,
         o_ref, ext_ref, y_ref) = refs
        ext_ref[0:halo, :] = jnp.zeros((halo, ext_ref.shape[1]), F32)
    ext_ref[halo:halo + tm, :] = u_ref[...]
    d = u_ref.shape[-1]
    off = halo - (width - 1)

    def col_body(c, carry):
        c0 = pl.multiple_of(c * LANES, LANES)
        cols = pl.ds(c0, LANES)
        wk = [jnp.broadcast_to(wdw_ref[pl.ds(k, 1), cols], (8, LANES)) for k in range(width)]
        bias = bdw_ref[:, cols]
        for r in range(tm // 8):
            acc = wk[0] * ext_ref[pl.ds(r * 8 + off, 8), cols]
            for k in range(1, width):
                acc = acc + wk[k] * ext_ref[pl.ds(r * 8 + off + k, 8), cols]
            y_ref[pl.ds(r * 8, 8), cols] = acc + bias
        return carry

    lax.fori_loop(0, d // LANES, col_body, 0)
    o_ref[...] = _ln_swish_pw2(y_ref[...], h_ref[...], lng_ref, lnb_ref, w2_ref, b2_ref)


def _a2_seq_call(u, h, prev, wdw, bdw, lng, lnb, w2, b2, tm):
    bsz, t, d = u.shape
    width = wdw.shape[0]
    halo = 32
    assert halo >= width - 1 and (tm % halo == 0 or prev is None)
    row = pl.BlockSpec((None, tm, d), lambda bi, i: (bi, i, 0))
    consts = [_const((width, d)), _const((1, d)), _const((1, d)), _const((1, d)), _const((d, d)), _const((1, d))]
    if prev is not None:
        k = tm // halo
        in_specs = [row, pl.BlockSpec((None, halo, d), lambda bi, i: (bi, jnp.maximum(i * k - 1, 0), 0)),
                    _const((halo, d)), row] + consts
        args = (u, u, prev, h, wdw, bdw, lng, lnb, w2, b2)
    else:
        assert t == tm
        in_specs = [row, row] + consts
        args = (u, h, wdw, bdw, lng, lnb, w2, b2)
    return pl.pallas_call(
        functools.partial(_a2_seq_kernel, tm=tm, halo=halo, width=width, has_prev=prev is not None),
        grid=(bsz, t // tm),
        in_specs=in_specs,
        out_specs=row,
        out_shape=jax.ShapeDtypeStruct((bsz, t, d), F32),
        scratch_shapes=[pltpu.VMEM((halo + tm, d), F32), pltpu.VMEM((tm, d), F32)],
        compiler_params=_cparams(("parallel", "parallel")),
        name="conv_dw_ln_pw2",
    )(*args)


def _a2_tm_kernel(xp_ref, h_ref, wdw_ref, bdw_ref, lng_ref, lnb_ref, w2_ref, b2_ref, o_ref, y_ref, *, width, cw):
    nt, bs, d = h_ref.shape
    for c in range(d // cw):
        cols = slice(c * cw, (c + 1) * cw)
        wk = [jnp.broadcast_to(wdw_ref[k:k + 1, cols], (bs, cw)) for k in range(width)]
        bias = bdw_ref[:, cols]
        for t in range(nt):
            acc = wk[0] * xp_ref[t, :, cols]
            for k in range(1, width):
                acc = acc + wk[k] * xp_ref[t + k, :, cols]
            y_ref[t * bs:(t + 1) * bs, cols] = acc + bias
    h = h_ref[...].reshape(nt * bs, d)
    out = _ln_swish_pw2(y_ref[...], h, lng_ref, lnb_ref, w2_ref, b2_ref)
    o_ref[...] = out.reshape(nt, bs, d)


def _a2_tm_call(xp, h, wdw, bdw, lng, lnb, w2, b2, bs):
    nt, db, d = h.shape
    width = wdw.shape[0]
    return pl.pallas_call(
        functools.partial(_a2_tm_kernel, width=width, cw=2 * LANES),
        grid=(db // bs,),
        in_specs=[pl.BlockSpec((nt + width - 1, bs, d), lambda i: (0, i, 0)),
                  pl.BlockSpec((nt, bs, d), lambda i: (0, i, 0)),
                  _const((width, d)), _const((1, d)), _const((1, d)), _const((1, d)), _const((d, d)), _const((1, d))],
        out_specs=pl.BlockSpec((nt, bs, d), lambda i: (0, i, 0)),
        out_shape=jax.ShapeDtypeStruct((nt, db, d), F32),
        scratch_shapes=[pltpu.VMEM((nt * bs, d), F32)],
        compiler_params=_cparams(("parallel",)),
        name="conv_dw_ln_pw2_sample",
    )(xp, h, wdw, bdw, lng, lnb, w2, b2)


def _ffn_kernel(*refs, mode, tm, fc, shift, ngroups):
    if mode == "prev":
        x_ref, xh_ref, prev_ref, g_ref, wup_ref, wdw_ref, wdn_ref, o_ref, tail_ref = refs
    elif mode == "zero":
        x_ref, g_ref, wup_ref, wdw_ref, wdn_ref, o_ref, tail_ref = refs
    else:
        x_ref, st_ref, g_ref, wup_ref, wdw_ref, wdn_ref, o_ref, tail_ref = refs
    x = x_ref[...]
    d = x.shape[-1]
    f = wdn_ref.shape[0]
    if mode == "prev":
        first = pl.program_id(1) == 0
        halo = jnp.where(first, prev_ref[...], xh_ref[...])
        xe = jnp.concatenate([halo, x], axis=0)
    elif mode == "zero":
        xe = jnp.concatenate([jnp.zeros((8, d), F32), x], axis=0)
    else:
        xe = x
    pre = xe.shape[0] - tm if mode != "state" else 2 * shift
    xn = (xe * _rms_scale(xe) * g_ref[...]).astype(BF16)
    ntail = tail_ref.shape[1]
    fg = wdn_ref.shape[0]
    acc = jnp.zeros((tm, d), F32)
    for j in range(fg // fc):
        cols = slice(j * fc, (j + 1) * fc)
        act = None
        for gu in range(2):
            hu = _dot(xn, wup_ref[gu, :, cols])
            tail_ref[gu, :, cols] = hu[hu.shape[0] - ntail:, :]
            if mode == "state":
                hu = jnp.concatenate([st_ref[gu, 0, :, cols], st_ref[gu, 1, :, cols], hu], axis=0)
            w = wdw_ref[gu, :, cols]
            conv = (w[0:1] * hu[pre - 2 * shift:pre - 2 * shift + tm]
                    + w[1:2] * hu[pre - shift:pre - shift + tm]
                    + w[2:3] * hu[pre:pre + tm])
            act = conv * _sigmoid(conv) if gu == 0 else act * conv
        acc = acc + _dot(act.astype(BF16), wdn_ref[cols, :])
    if ngroups == 1:
        o_ref[...] = x + acc
    else:
        gj = pl.program_id(2)

        @pl.when(gj == 0)
        def _():
            o_ref[...] = x + acc

        @pl.when(gj > 0)
        def _():
            o_ref[...] = o_ref[...] + acc


def _ffn_call(x, g, wup, wdw, wdn, tm, prev=None, state=None):
    bsz, t, d = x.shape
    f = wdn.shape[0]
    fc = FFN_CHUNK if f % FFN_CHUNK == 0 else LANES
    fg = fc if state is not None else f
    ngroups = f // fg
    row = pl.BlockSpec((None, tm, d), lambda bi, i, gj: (bi, i, 0))
    consts = [pl.BlockSpec((1, d), lambda bi, i, gj: (0, 0)),
              pl.BlockSpec((2, d, fg), lambda bi, i, gj: (0, 0, gj)),
              pl.BlockSpec((2, 3, fg), lambda bi, i, gj: (0, 0, gj)),
              pl.BlockSpec((fg, d), lambda bi, i, gj: (gj, 0))]
    if state is not None:
        assert bsz == 1 and t == tm
        mode, shift, ntail = "state", state.shape[2], 2 * state.shape[2]
        in_specs = [row, pl.BlockSpec(state.shape[:3] + (fg,), lambda bi, i, gj: (0, 0, 0, gj))] + consts
        args = (x, state, g, wup, wdw, wdn)
    elif prev is not None:
        mode, shift, ntail = "prev", 1, 8
        k = tm // 8
        in_specs = [row, pl.BlockSpec((None, 8, d), lambda bi, i, gj: (bi, jnp.maximum(i * k - 1, 0), 0)),
                    pl.BlockSpec((8, d), lambda bi, i, gj: (0, 0))] + consts
        args = (x, x, prev, g, wup, wdw, wdn)
    else:
        assert t == tm
        mode, shift, ntail = "zero", 1, 8
        in_specs = [row] + consts
        args = (x, g, wup, wdw, wdn)
    out, tail = pl.pallas_call(
        functools.partial(_ffn_kernel, mode=mode, tm=tm, fc=fc, shift=shift, ngroups=ngroups),
        grid=(bsz, t // tm, ngroups),
        in_specs=in_specs,
        out_specs=[row, pl.BlockSpec((None, 2, ntail, fg), lambda bi, i, gj: (bi, 0, 0, gj))],
        out_shape=[jax.ShapeDtypeStruct((bsz, t, d), F32), jax.ShapeDtypeStruct((bsz, 2, ntail, f), F32)],
        compiler_params=_cparams(("parallel", "arbitrary", "arbitrary")),
        name="conv_ffn_" + mode,
    )(*args)
    return out, tail


def _wo_kernel(h_ref, o_ref, w_ref, out_ref):
    out_ref[...] = h_ref[...] + _dot(o_ref[...], w_ref[...])


def _wo_call(h, o, w, tm):
    bsz, t, d = h.shape
    hv = o.shape[-1]
    return pl.pallas_call(
        _wo_kernel,
        grid=(bsz, t // tm),
        in_specs=[pl.BlockSpec((None, tm, d), lambda bi, i: (bi, i, 0)),
                  pl.BlockSpec((None, tm, hv), lambda bi, i: (bi, i, 0)), _const((hv, d))],
        out_specs=pl.BlockSpec((None, tm, d), lambda bi, i: (bi, i, 0)),
        out_shape=jax.ShapeDtypeStruct((bsz, t, d), F32),
        compiler_params=_cparams(("parallel", "parallel")),
        name="attn_out_proj",
    )(h, o, w)


def _proj_kernel(h_ref, cs_ref, sn_ref, gkv_ref, gmix_ref, wkv_ref, glat_ref, gkr_ref, wdq_ref, gql_ref,
                 wqm_ref, wqr_ref, gq_ref, wuk_ref, gk_ref, wuv_ref,
                 c_ref, kr_ref, q_ref, k_ref, v_ref, *, nheads, nope, rope, scale):
    h = h_ref[...]
    hs = h * _rms_scale(h)
    hn = (hs * gkv_ref[...]).astype(BF16)
    xq = (hs * gmix_ref[...]).astype(BF16)
    z = _dot(hn, wkv_ref[...])
    cpre, a, ar = z[:, :LANES], z[:, LANES:2 * LANES], z[:, 2 * LANES:3 * LANES]
    c = cpre * _rms_scale(cpre) * glat_ref[...]
    c_ref[...] = c
    cs = cs_ref[...]
    sn = sn_ref[...]
    ra = lax.rsqrt(jnp.sum(a * a, axis=-1, keepdims=True) * (1.0 / rope) + NORM_EPS)
    kr = ra * (a * gkr_ref[...] * cs + ar * sn)
    kr_ref[...] = kr
    cb = c.astype(BF16)
    v_ref[...] = _dot(cb, wuv_ref[...]).astype(BF16)
    kn = _dot(cb, wuk_ref[...])
    ql = _dot(xq, wdq_ref[...])
    ql = (ql * _rms_scale(ql) * gql_ref[...]).astype(BF16)
    qm = _dot(ql, wqm_ref[...])
    qr = _dot(ql, wqr_ref[...])
    lane = lax.broadcasted_iota(jnp.int32, (1, HEAD_SLOT), 1)
    mn = (lane < nope).astype(F32)
    mr = jnp.logical_and(lane >= nope, lane < nope + rope).astype(F32)
    gq = gq_ref[...]
    gk = gk_ref[...]
    for hd in range(nheads):
        sl = slice(hd * HEAD_SLOT, (hd + 1) * HEAD_SLOT)
        qh = qm[:, sl]
        sq = qh * qh
        rn = lax.rsqrt(jnp.sum(sq * mn, axis=-1, keepdims=True) * (1.0 / nope) + NORM_EPS)
        rr = lax.rsqrt(jnp.sum(sq * mr, axis=-1, keepdims=True) * (1.0 / rope) + NORM_EPS)
        qv = (qh * gq * cs + qr[:, sl] * sn) * (mn * rn + mr * rr) * scale
        q_ref[:, sl] = qv.astype(BF16)
        kh = kn[:, sl]
        rk = lax.rsqrt(jnp.sum(kh * kh, axis=-1, keepdims=True) * (1.0 / nope) + NORM_EPS)
        k_ref[:, sl] = (kh * rk * gk + kr).astype(BF16)


def _proj_call(h, cs, sn, w, tm, nheads, nope, rope, vdim):
    bsz, t, d = h.shape
    ql = w["wdq"].shape[1]
    hs = nheads * HEAD_SLOT
    row = lambda n: pl.BlockSpec((None, tm, n), lambda bi, i: (bi, i, 0))
    tab = pl.BlockSpec((tm, HEAD_SLOT), lambda bi, i: (i, 0))
    in_specs = [row(d), tab, tab, _const((1, d)), _const((1, d)), _const((d, 3 * LANES)), _const((1, LANES)),
                _const((1, HEAD_SLOT)), _const((d, ql)), _const((1, ql)), _const((ql, hs)), _const((ql, hs)),
                _const((1, HEAD_SLOT)), _const((LANES, hs)), _const((1, HEAD_SLOT)), _const((LANES, nheads * vdim))]
    return pl.pallas_call(
        functools.partial(_proj_kernel, nheads=nheads, nope=nope, rope=rope,
                          scale=1.0 / math.sqrt(nope + rope)),
        grid=(bsz, t // tm),
        in_specs=in_specs,
        out_specs=[row(LANES), row(HEAD_SLOT), row(hs), row(hs), row(nheads * vdim)],
        out_shape=[jax.ShapeDtypeStruct((bsz, t, LANES), F32), jax.ShapeDtypeStruct((bsz, t, HEAD_SLOT), F32),
                   jax.ShapeDtypeStruct((bsz, t, hs), BF16), jax.ShapeDtypeStruct((bsz, t, hs), BF16),
                   jax.ShapeDtypeStruct((bsz, t, nheads * vdim), BF16)],
        compiler_params=_cparams(("parallel", "parallel")),
        name="mla_proj",
    )(h, cs, sn, w["gkv"], w["gmix"], w["wkv"], w["glat"], w["gkr"], w["wdq"], w["gql"], w["wqm"], w["wqr"],
      w["gq"], w["wuk"], w["gk"], w["wuv"])


def _attn_update(hd, r0, nr, q_ref, kb, vb, mask, m_ref, l_ref, acc_ref, first):
    rows = slice(r0, r0 + nr)
    q = q_ref[rows, hd * HEAD_SLOT:(hd + 1) * HEAD_SLOT]
    s = _dot_nt(q, kb)
    if mask is not None:
        s = jnp.where(mask, s, NEG)
    nk = s.shape[1]
    m_cur = jnp.max(s, axis=1, keepdims=True)
    if first:
        m_new = jnp.broadcast_to(m_cur, (nr, LANES))
    else:
        m_prev = m_ref[hd, rows, :]
        m_new = jnp.maximum(m_prev, m_cur)
    if nk % LANES == 0:
        mb = jnp.concatenate([m_new] * (nk // LANES), axis=1) if nk > LANES else m_new
    else:
        mb = m_new[:, :nk]
    p = jnp.exp(s - mb)
    ps = jnp.sum(p, axis=1, keepdims=True)
    pv = _dot(p.astype(BF16), vb)
    if first:
        l_ref[hd, rows, :] = jnp.broadcast_to(ps, (nr, LANES))
        acc_ref[hd, rows, :] = pv
    else:
        alpha = jnp.exp(m_prev - m_new)
        l_ref[hd, rows, :] = alpha * l_ref[hd, rows, :] + ps
        acc_ref[hd, rows, :] = alpha * acc_ref[hd, rows, :] + pv
    m_ref[hd, rows, :] = m_new


def _attn_kernel(*refs, tq, tk, dk, has_prefix, vdim):
    if has_prefix:
        q_ref, k_ref, v_ref, pk_ref, pv_ref, o_ref, m_ref, l_ref, acc_ref = refs
    else:
        q_ref, k_ref, v_ref, o_ref, m_ref, l_ref, acc_ref = refs
    i = pl.program_id(2)
    if has_prefix:
        for hd in range(2):
            _attn_update(hd, 0, tq, q_ref, pk_ref[:, hd * HEAD_SLOT:(hd + 1) * HEAD_SLOT], pv_ref[...], None,
                         m_ref, l_ref, acc_ref, True)
        nfull = i * (tq // tk)

        def body(j, carry):
            k0 = pl.multiple_of(j * tk, tk)
            vb = v_ref[pl.ds(k0, tk), :]
            for hd in range(2):
                _attn_update(hd, 0, tq, q_ref, k_ref[pl.ds(k0, tk), hd * HEAD_SLOT:(hd + 1) * HEAD_SLOT], vb, None,
                             m_ref, l_ref, acc_ref, False)
            return carry

        lax.fori_loop(0, nfull, body, 0)
    for jj in range(tq // dk):
        r0 = jj * dk
        nr = tq - r0
        k0 = pl.multiple_of(i * tq + r0, dk)
        row = lax.broadcasted_iota(jnp.int32, (nr, dk), 0)
        col = lax.broadcasted_iota(jnp.int32, (nr, dk), 1)
        mask = col <= row
        vb = v_ref[pl.ds(k0, dk), :]
        for hd in range(2):
            _attn_update(hd, r0, nr, q_ref, k_ref[pl.ds(k0, dk), hd * HEAD_SLOT:(hd + 1) * HEAD_SLOT], vb, mask,
                         m_ref, l_ref, acc_ref, (not has_prefix) and jj == 0)
    lane = lax.broadcasted_iota(jnp.int32, (tq, LANES), 1)
    o0 = acc_ref[0] / l_ref[0]
    o1 = acc_ref[1] / l_ref[1]
    o_ref[...] = jnp.where(lane < vdim, o0, o1).astype(o_ref.dtype)


def _attn_call(q, k, v, pk, pv, tq, tk, dk, nheads, vdim):
    bsz, t, _ = q.shape
    assert 2 * vdim == LANES and nheads % 2 == 0 and t % tq == 0 and tq % tk == 0 and tq % dk == 0
    has_prefix = pk is not None
    assert has_prefix or t == tq
    in_specs = [pl.BlockSpec((None, tq, 2 * HEAD_SLOT), lambda bi, hp, i: (bi, i, hp)),
                pl.BlockSpec((None, t, 2 * HEAD_SLOT), lambda bi, hp, i: (bi, 0, hp)),
                pl.BlockSpec((None, t, LANES), lambda bi, hp, i: (bi, 0, hp))]
    args = [q, k, v]
    if has_prefix:
        npre = pk.shape[0]
        in_specs += [pl.BlockSpec((npre, 2 * HEAD_SLOT), lambda bi, hp, i: (0, hp)),
                     pl.BlockSpec((npre, LANES), lambda bi, hp, i: (0, hp))]
        args += [pk, pv]
    return pl.pallas_call(
        functools.partial(_attn_kernel, tq=tq, tk=tk, dk=dk, has_prefix=has_prefix, vdim=vdim),
        grid=(bsz, nheads // 2, t // tq),
        in_specs=in_specs,
        out_specs=pl.BlockSpec((None, tq, LANES), lambda bi, hp, i: (bi, i, hp)),
        out_shape=jax.ShapeDtypeStruct((bsz, t, nheads * vdim), BF16),
        scratch_shapes=[pltpu.VMEM((2, tq, LANES), F32)] * 3,
        compiler_params=_cparams(("parallel", "parallel", "arbitrary")),
        name="mla_prompt_attn",
    )(*args)


def _qt_kernel(q_ref, w_ref, o_ref, *, nheads):
    for hd in range(nheads):
        o_ref[hd] = _dot(q_ref[:, hd * HEAD_SLOT:(hd + 1) * HEAD_SLOT], w_ref[hd]).astype(BF16)


def _qt_call(q, w, nheads):
    r = q.shape[0]
    return pl.pallas_call(
        functools.partial(_qt_kernel, nheads=nheads),
        out_shape=jax.ShapeDtypeStruct((nheads, r, LANES), BF16),
        compiler_params=pltpu.CompilerParams(vmem_limit_bytes=VMEM_LIMIT),
        name="mla_absorb_q",
    )(q, w)


def _uv_kernel(ol_ref, w_ref, o_ref, *, nheads):
    for hp in range(nheads // 2):
        r = (_dot(ol_ref[2 * hp].astype(BF16), w_ref[2 * hp])
             + _dot(ol_ref[2 * hp + 1].astype(BF16), w_ref[2 * hp + 1]))
        o_ref[:, hp * LANES:(hp + 1) * LANES] = r.astype(BF16)


def _uv_call(ol, w, nheads, vdim):
    r = ol.shape[1]
    return pl.pallas_call(
        functools.partial(_uv_kernel, nheads=nheads),
        out_shape=jax.ShapeDtypeStruct((r, nheads * vdim), BF16),
        compiler_params=pltpu.CompilerParams(vmem_limit_bytes=VMEM_LIMIT),
        name="mla_value_up",
    )(ol, w)


def _paged_scores(wukt_ref, qt_ref, qr_ref, cb, krb, nheads, nq, nope):
    n = cb.shape[0]
    kt = _dot_nt(wukt_ref[...], cb)
    ss = jnp.sum((kt * kt).reshape(nheads, nope, n), axis=1)
    rs = lax.rsqrt(ss * (1.0 / nope) + NORM_EPS)
    sraw = _dot_nt(qt_ref[...], cb)
    srope = _dot_nt(qr_ref[...], krb)
    return (sraw.reshape(nq, nheads, n) * rs[None, :, :]).reshape(nq * nheads, n) + srope


def _paged_kernel(pt_ref, *refs, npg, nheads, nq, nope):
    lat_refs, kr_refs = refs[:npg], refs[npg:2 * npg]
    wukt_ref, qt_ref, qr_ref, cn_ref, krn_ref, o_ref, m_ref, l_ref, acc_ref = refs[2 * npg:]
    j = pl.program_id(1)
    nrow = nq * nheads

    @pl.when(j == 0)
    def _():
        cb = cn_ref[...].astype(BF16)
        s = _paged_scores(wukt_ref, qt_ref, qr_ref, cb, krn_ref[...].astype(BF16), nheads, nq, nope)
        n = s.shape[1]
        row = lax.broadcasted_iota(jnp.int32, (nrow, n), 0)
        col = lax.broadcasted_iota(jnp.int32, (nrow, n), 1)
        s = jnp.where(col * nheads <= row, s, NEG)
        m = jnp.max(s, axis=1, keepdims=True)
        p = jnp.exp(s - m)
        m_ref[...] = jnp.broadcast_to(m, (nrow, LANES))
        l_ref[...] = jnp.broadcast_to(jnp.sum(p, axis=1, keepdims=True), (nrow, LANES))
        acc_ref[...] = _dot(p.astype(BF16), cb)

    cb = jnp.concatenate([r[...] for r in lat_refs], axis=0).astype(BF16)
    krb = jnp.concatenate([r[...] for r in kr_refs], axis=0).astype(BF16)
    s = _paged_scores(wukt_ref, qt_ref, qr_ref, cb, krb, nheads, nq, nope)
    n = s.shape[1]
    m_prev = m_ref[...]
    m_new = jnp.maximum(m_prev, jnp.max(s, axis=1, keepdims=True))
    p = jnp.exp(s - jnp.concatenate([m_new] * (n // LANES), axis=1))
    alpha = jnp.exp(m_prev - m_new)
    l_ref[...] = alpha * l_ref[...] + jnp.sum(p, axis=1, keepdims=True)
    acc_ref[...] = alpha * acc_ref[...] + _dot(p.astype(BF16), cb)
    m_ref[...] = m_new

    @pl.when(j == pl.num_programs(1) - 1)
    def _():
        o_ref[...] = acc_ref[...] / l_ref[...]


def _paged_call(page_table, cache_lat, cache_kr, wukt, qt, qr, cn, krn, nheads, nq, nope):
    db, npages = page_table.shape
    _, page, clat = cache_lat.shape
    rdim = cache_kr.shape[2]
    npg = PAGES_PER_STEP if npages % PAGES_PER_STEP == 0 else 1
    nrow = nq * nheads
    nnew = cn.shape[1]

    def page_map(p):
        return lambda b, j, pt: (pt[b * npages + j * npg + p], 0, 0)

    per_b = lambda shape: pl.BlockSpec((None,) + shape, lambda b, j, pt: (b, 0, 0))
    in_specs = ([pl.BlockSpec((None, page, clat), page_map(p)) for p in range(npg)]
                + [pl.BlockSpec((None, page, rdim), page_map(p)) for p in range(npg)]
                + [pl.BlockSpec(wukt.shape, lambda b, j, pt: (0, 0)),
                   per_b((nrow, clat)), per_b((nrow, rdim)), per_b((nnew, clat)), per_b((nnew, rdim))])
    return pl.pallas_call(
        functools.partial(_paged_kernel, npg=npg, nheads=nheads, nq=nq, nope=nope),
        grid_spec=pltpu.PrefetchScalarGridSpec(
            num_scalar_prefetch=1,
            grid=(db, npages // npg),
            in_specs=in_specs,
            out_specs=per_b((nrow, clat)),
            scratch_shapes=[pltpu.VMEM((nrow, LANES), F32)] * 3),
        out_shape=jax.ShapeDtypeStruct((db, nrow, clat), F32),
        compiler_params=_cparams(("parallel", "arbitrary")),
        name="mla_paged_attn",
    )(page_table.reshape(-1), *([cache_lat] * npg), *([cache_kr] * npg), wukt, qt, qr, cn, krn)


def _slot(x, lo, width=HEAD_SLOT):
    n = x.shape[-1]
    pad = [(0, 0)] * (x.ndim - 1) + [(lo, width - lo - n)]
    return jnp.pad(x, pad)


def _half_rot_cols(w):
    half = w.shape[-1] // 2
    return jnp.concatenate([-w[..., half:], w[..., :half]], axis=-1)


def _rope_tables(pos, rope, nope):
    half = rope // 2
    inv = ROPE_THETA ** (-jnp.arange(half, dtype=F32) / half)
    ang = pos.astype(F32)[:, None] * inv[None, :]
    cos, sin = jnp.cos(ang), jnp.sin(ang)
    cs = jnp.concatenate([jnp.ones((pos.shape[0], nope), F32), cos, cos], axis=1)
    sn = jnp.concatenate([jnp.zeros((pos.shape[0], nope), F32), sin, sin], axis=1)
    return _slot(cs, 0), _slot(sn, 0)


def kernel(x_prompt, x_sample, state_conv_a, state_ffn_conv, cache_kv_latent, cache_k_rope, page_table, meta_tokens, norm_mix, norm_ffn, a_w_pw1, a_b_pw1, a_w_dw, a_b_dw, a_ln_g, a_ln_b, a_w_pw2, a_b_pw2, ffn_w_up, ffn_w_dw, ffn_w_down, kv_norm, mla_w_dkv, mla_lat_norm, mla_w_kr, mla_knorm_rope, mla_w_uk, mla_w_uv, mla_knorm_nope, mla_w_dq, mla_q_lat_norm, mla_w_uq, mla_qnorm_nope, mla_qnorm_rope, mla_w_o):
    bsz, seq, d = x_prompt.shape
    db, nq, _ = x_sample.shape
    nmeta = meta_tokens.shape[0]
    depth = norm_mix.shape[0]
    n_a = a_w_pw1.shape[0]
    n_b = mla_w_dq.shape[0]
    f = ffn_w_down.shape[1]
    clat, nheads, nope = mla_w_uk.shape
    rope = mla_w_kr.shape[1]
    vdim = mla_w_uv.shape[2]
    width_a = a_w_dw.shape[1]
    npages, page = page_table.shape[1], cache_kv_latent.shape[1]
    past_len = npages * page
    assert n_b == 1 and depth == n_a + n_b and clat == LANES and nope + rope <= HEAD_SLOT
    assert seq >= width_a - 1 and nmeta % 16 == 0 and nmeta >= 8
    tm = ROW_TILE if seq % ROW_TILE == 0 else seq
    rs = nq * db
    row2 = lambda v: v.reshape(1, -1)

    hp = x_prompt
    hm = meta_tokens.astype(F32)[None]
    hs = jnp.transpose(x_sample, (1, 0, 2)).reshape(1, rs, d)

    conv_a_p, conv_a_s, ffn_p, ffn_s = [], [], [], []

    def run_ffn(layer, hp, hm, hs):
        g = row2(norm_ffn[layer])
        wup = jnp.transpose(ffn_w_up[layer].reshape(d, 2, f), (1, 0, 2)).astype(BF16)
        wdw = jnp.transpose(ffn_w_dw[layer].reshape(3, 2, f), (1, 0, 2))
        wdn = ffn_w_down[layer].astype(BF16)
        st = jnp.transpose(state_ffn_conv[layer].reshape(db, 2, 2, f), (2, 1, 0, 3))
        hm_new, _ = _ffn_call(hm, g, wup, wdw, wdn, nmeta)
        hp_new, tail_p = _ffn_call(hp, g, wup, wdw, wdn, tm, prev=hm[0, nmeta - 8:])
        hs_new, tail_s = _ffn_call(hs, g, wup, wdw, wdn, rs, state=st)
        ffn_p.append(jnp.transpose(tail_p[:, :, 6:8, :], (0, 2, 1, 3)).reshape(bsz, 2, 2 * f))
        ffn_s.append(jnp.transpose(tail_s[0].reshape(2, 2, db, f), (2, 1, 0, 3)).reshape(db, 2, 2 * f))
        return hp_new, hm_new, hs_new

    for layer in range(n_a):
        g = row2(norm_mix[layer])
        w1 = a_w_pw1[layer].astype(BF16)
        b1 = row2(a_b_pw1[layer])
        conv_w = (a_w_dw[layer], row2(a_b_dw[layer]), row2(a_ln_g[layer]), row2(a_ln_b[layer]),
                  a_w_pw2[layer].astype(BF16), row2(a_b_pw2[layer]))
        um = _a1_call(hm, g, w1, b1, nmeta)
        up = _a1_call(hp, g, w1, b1, tm)
        us = _a1_call(hs, g, w1, b1, rs)
        halo = jnp.pad(um[0], ((32 - nmeta, 0), (0, 0))) if nmeta < 32 else um[0, nmeta - 32:]
        hm_new = _a2_seq_call(um, hm, None, *conv_w, nmeta)
        hp = _a2_seq_call(up, hp, halo, *conv_w, tm)
        hm = hm_new
        xp_s = jnp.concatenate([jnp.transpose(state_conv_a[layer], (1, 0, 2)), us.reshape(nq, db, d)], axis=0)
        hs = _a2_tm_call(xp_s, hs.reshape(nq, db, d), *conv_w, 32 if db % 32 == 0 else db).reshape(1, rs, d)
        conv_a_p.append(up[:, seq - (width_a - 1):])
        conv_a_s.append(jnp.transpose(xp_s[nq:], (1, 0, 2)))
        hp, hm, hs = run_ffn(layer, hp, hm, hs)

    layer = n_a
    g_kr = mla_knorm_rope
    wkr_g = mla_w_kr * g_kr[None, :]
    uq = mla_w_uq[0]
    uq_rope_g = uq[:, :, nope:] * mla_qnorm_rope[0][None, None, :]
    pw = {
        "gkv": row2(kv_norm), "gmix": row2(norm_mix[layer]),
        "wkv": jnp.concatenate([mla_w_dkv, _slot(mla_w_kr, nope), _slot(_half_rot_cols(wkr_g), nope)], axis=1).astype(BF16),
        "glat": row2(mla_lat_norm), "gkr": _slot(row2(g_kr), nope),
        "wdq": mla_w_dq[0].astype(BF16), "gql": row2(mla_q_lat_norm[0]),
        "wqm": _slot(uq, 0).reshape(uq.shape[0], nheads * HEAD_SLOT).astype(BF16),
        "wqr": _slot(_half_rot_cols(uq_rope_g), nope).reshape(uq.shape[0], nheads * HEAD_SLOT).astype(BF16),
        "gq": _slot(row2(jnp.concatenate([mla_qnorm_nope[0], mla_qnorm_rope[0]])), 0),
        "wuk": _slot(mla_w_uk, 0).reshape(clat, nheads * HEAD_SLOT).astype(BF16),
        "gk": _slot(row2(mla_knorm_nope), 0),
        "wuv": mla_w_uv.reshape(clat, nheads * vdim).astype(BF16),
    }
    cs_m, sn_m = _rope_tables(jnp.arange(nmeta), rope, nope)
    cs_p, sn_p = _rope_tables(nmeta + jnp.arange(seq), rope, nope)
    cs_s, sn_s = _rope_tables(jnp.repeat(past_len + jnp.arange(nq), db), rope, nope)
    proj = functools.partial(_proj_call, w=pw, nheads=nheads, nope=nope, rope=rope, vdim=vdim)
    c_m, kr_m, q_m, k_m, v_m = proj(hm, cs_m, sn_m, tm=nmeta)
    c_p, kr_p, q_p, k_p, v_p = proj(hp, cs_p, sn_p, tm=tm)
    c_s, kr_s, q_s, _, _ = proj(hs, cs_s, sn_s, tm=rs)
    kr_m, kr_p, kr_s = (x[..., nope:nope + rope] for x in (kr_m, kr_p, kr_s))

    tq = ATTN_Q_TILE if seq % ATTN_Q_TILE == 0 else seq
    tk = ATTN_K_TILE if tq % ATTN_K_TILE == 0 else tq
    dk = ATTN_DIAG_TILE if tq % ATTN_DIAG_TILE == 0 else tq
    o_m = _attn_call(q_m, k_m, v_m, None, None, nmeta, nmeta, nmeta, nheads, vdim)
    o_p = _attn_call(q_p, k_p, v_p, k_m[0], v_m[0], tq, tk, dk, nheads, vdim)

    wukt = jnp.transpose(mla_w_uk.reshape(clat, nheads * nope)).astype(BF16)
    wqt = jnp.transpose(mla_w_uk, (1, 2, 0)) * mla_knorm_nope[None, :, None]
    wqt = jnp.pad(wqt, ((0, 0), (0, HEAD_SLOT - nope), (0, 0))).astype(BF16)
    qt = _qt_call(q_s[0], wqt, nheads)
    qt = jnp.transpose(qt.reshape(nheads, nq, db, clat), (2, 1, 0, 3)).reshape(db, nq * nheads, clat)
    qr = jnp.transpose(q_s[0].reshape(nq, db, nheads, HEAD_SLOT)[..., nope:nope + rope], (1, 0, 2, 3))
    qr = qr.reshape(db, nq * nheads, rope)
    nnew = LANES
    cn = jnp.pad(jnp.transpose(c_s[0].reshape(nq, db, clat), (1, 0, 2)), ((0, 0), (0, nnew - nq), (0, 0)))
    krn = jnp.pad(jnp.transpose(kr_s[0].reshape(nq, db, rope), (1, 0, 2)), ((0, 0), (0, nnew - nq), (0, 0)))
    o_lat = _paged_call(page_table, cache_kv_latent, cache_k_rope, wukt, qt, qr, cn, krn, nheads, nq, nope)
    ol = jnp.transpose(o_lat.reshape(db, nq, nheads, clat), (2, 1, 0, 3)).reshape(nheads, rs, clat)
    wuv_pad = jnp.stack([_slot(mla_w_uv[:, hd, :], (hd % 2) * vdim, LANES) for hd in range(nheads)]).astype(BF16)
    o_s = _uv_call(ol, wuv_pad, nheads, vdim)[None]

    wo = mla_w_o[0].astype(BF16)
    hm = _wo_call(hm, o_m, wo, nmeta)
    hp = _wo_call(hp, o_p, wo, tm)
    hs = _wo_call(hs, o_s, wo, rs)
    hp, hm, hs = run_ffn(layer, hp, hm, hs)

    y_sample = jnp.transpose(hs.reshape(nq, db, d), (1, 0, 2))
    kv_lat_p = jnp.concatenate([jnp.broadcast_to(c_m, (bsz, nmeta, clat)), c_p], axis=1)
    k_rope_p = jnp.concatenate([jnp.broadcast_to(kr_m, (bsz, nmeta, rope)), kr_p], axis=1)
    kv_lat_s = jnp.transpose(c_s[0].reshape(nq, db, clat), (1, 0, 2))
    k_rope_s = jnp.transpose(kr_s[0].reshape(nq, db, rope), (1, 0, 2))
    return (hp, y_sample, jnp.stack(conv_a_p), jnp.stack(conv_a_s), jnp.stack(ffn_p), jnp.stack(ffn_s),
            kv_lat_p, k_rope_p, kv_lat_s, k_rope_s)
```

```python
import functools
import math

import jax
import jax.numpy as jnp
from jax import lax
from jax.experimental import pallas as pl
from jax.experimental.pallas import tpu as pltpu

F32 = jnp.float32
BF16 = jnp.bfloat16
NORM_EPS = 1e-6
ROPE_THETA = 10000.0
NEG = -1e30

LANES = 128
HEAD_SLOT = 128
ROW_TILE = 512
ATTN_Q_TILE = 512
ATTN_K_TILE = 512
ATTN_DIAG_TILE = 256
FFN_CHUNK = 256
ATTN_HEADS_PER_STEP = 4
PAGES_PER_CHUNK = 16
PAGED_SUB_KEYS = 1024
LOG2E = 1.4426950408889634
VMEM_LIMIT = 56 * 1024 * 1024


def _cparams(sem):
    return pltpu.CompilerParams(dimension_semantics=sem, vmem_limit_bytes=VMEM_LIMIT)


def _const(shape):
    zeros = (0,) * len(shape)
    return pl.BlockSpec(shape, lambda *_: zeros)


def _sigmoid(x):
    return 1.0 / (1.0 + jnp.exp(-x))


def _rms_scale(x):
    return lax.rsqrt(jnp.mean(x * x, axis=-1, keepdims=True) + NORM_EPS)


def _dot(a, b):
    return jnp.dot(a, b, preferred_element_type=F32)


def _dot_nt(a, b):
    return lax.dot_general(a, b, (((1,), (1,)), ((), ())), preferred_element_type=F32)


def _a1_kernel(x_ref, g_ref, w_ref, b_ref, u_ref):
    x = x_ref[...]
    xn = x * _rms_scale(x) * g_ref[...]
    z = _dot(xn.astype(BF16), w_ref[...]) + b_ref[...]
    d = u_ref.shape[-1]
    u_ref[...] = z[:, :d] * _sigmoid(z[:, d:])


def _a1_call(x, g, w, b, tm):
    bsz, t, d = x.shape
    return pl.pallas_call(
        _a1_kernel,
        grid=(bsz, t // tm),
        in_specs=[pl.BlockSpec((None, tm, d), lambda bi, i: (bi, i, 0)),
                  _const((1, d)), _const((d, 2 * d)), _const((1, 2 * d))],
        out_specs=pl.BlockSpec((None, tm, d), lambda bi, i: (bi, i, 0)),
        out_shape=jax.ShapeDtypeStruct((bsz, t, d), F32),
        compiler_params=_cparams(("parallel", "parallel")),
        name="conv_pw1_glu",
    )(x, g, w, b)


def _ln_swish_pw2(y, h, lng_ref, lnb_ref, w2_ref, b2_ref):
    mu = jnp.mean(y, axis=-1, keepdims=True)
    yc = y - mu
    var = jnp.mean(yc * yc, axis=-1, keepdims=True)
    yn = yc * lax.rsqrt(var + NORM_EPS) * lng_ref[...] + lnb_ref[...]
    a = yn * _sigmoid(yn)
    return h + _dot(a.astype(BF16), w2_ref[...]) + b2_ref[...]


def _a2_seq_kernel(*refs, tm, halo, width, has_prev):
    if has_prev:
        (u_ref, uh_ref, prev_ref, h_ref, wdw_ref, bdw_ref, lng_ref, lnb_ref, w2_ref, b2_ref,
         o_ref, ext_ref, y_ref) = refs
        i = pl.program_id(1)

        @pl.when(i == 0)
        def _():
            ext_ref[0:halo, :] = prev_ref[...]

        @pl.when(i > 0)
        def _():
            ext_ref[0:halo, :] = uh_ref[...]
    else:
        (u_ref, h_ref, wdw_ref, bdw_ref, lng_ref, lnb_ref, w2_ref, b2_ref,
         o_ref, ext_ref, y_ref) = refs
        ext_ref[0:halo, :] = jnp.zeros((halo, ext_ref.shape[1]), F32)
    ext_ref[halo:halo + tm, :] = u_ref[...]
    d = u_ref.shape[-1]
    off = halo - (width - 1)

    taps = {}
    for k in range(width):
        taps.setdefault((off + k) % 8, []).append(((off + k) // 8, k))
    nblk = tm // 8

    def col_body(c, carry):
        c0 = pl.multiple_of(c * LANES, LANES)
        cols = pl.ds(c0, LANES)
        bias = jnp.broadcast_to(bdw_ref[:, cols], (8, LANES))
        sub = lax.broadcasted_iota(jnp.int32, (8, LANES), 0)
        xs = {}

        def xblk(jb):
            if jb not in xs:
                xs[jb] = ext_ref[pl.ds(jb * 8, 8), cols]
            return xs[jb]

        def zsum(r, jb):
            acc = None
            for a, k in taps[r]:
                term = wdw_ref[pl.ds(k * 8, 8), cols] * xblk(jb + a)
                acc = term if acc is None else acc + term
            return acc

        zprev = {r: zsum(r, 0) for r in taps if r != 0}
        for jb in range(nblk):
            y = zsum(0, jb) + bias if 0 in taps else bias
            znext = {}
            for r in zprev:
                znext[r] = zsum(r, jb + 1)
                y = y + pltpu.roll(jnp.where(sub >= r, zprev[r], znext[r]), 8 - r, axis=0)
            y_ref[pl.ds(jb * 8, 8), cols] = y
            zprev = znext
        return carry

    lax.fori_loop(0, d // LANES, col_body, 0)
    o_ref[...] = _ln_swish_pw2(y_ref[...], h_ref[...], lng_ref, lnb_ref, w2_ref, b2_ref)


def _a2_seq_call(u, h, prev, wdw, bdw, lng, lnb, w2, b2, tm):
    bsz, t, d = u.shape
    width = wdw.shape[0]
    halo = 32
    assert halo >= width - 1 and halo % 8 == 0 and (tm % halo == 0 or prev is None)
    wdw = jnp.repeat(wdw, 8, axis=0)
    row = pl.BlockSpec((None, tm, d), lambda bi, i: (bi, i, 0))
    consts = [_const((width * 8, d)), _const((1, d)), _const((1, d)), _const((1, d)), _const((d, d)), _const((1, d))]
    if prev is not None:
        k = tm // halo
        in_specs = [row, pl.BlockSpec((None, halo, d), lambda bi, i: (bi, jnp.maximum(i * k - 1, 0), 0)),
                    _const((halo, d)), row] + consts
        args = (u, u, prev, h, wdw, bdw, lng, lnb, w2, b2)
    else:
        assert t == tm
        in_specs = [row, row] + consts
        args = (u, h, wdw, bdw, lng, lnb, w2, b2)
    return pl.pallas_call(
        functools.partial(_a2_seq_kernel, tm=tm, halo=halo, width=width, has_prev=prev is not None),
        grid=(bsz, t // tm),
        in_specs=in_specs,
        out_specs=row,
        out_shape=jax.ShapeDtypeStruct((bsz, t, d), F32),
        scratch_shapes=[pltpu.VMEM((halo + tm, d), F32), pltpu.VMEM((tm, d), F32)],
        compiler_params=_cparams(("parallel", "parallel")),
        name="conv_dw_ln_pw2",
    )(*args)


def _a2_tm_kernel(xp_ref, h_ref, wdw_ref, bdw_ref, lng_ref, lnb_ref, w2_ref, b2_ref, o_ref, y_ref, *, width, cw):
    nt, bs, d = h_ref.shape
    for c in range(d // cw):
        cols = slice(c * cw, (c + 1) * cw)
        wk = [jnp.broadcast_to(wdw_ref[k:k + 1, cols], (bs, cw)) for k in range(width)]
        bias = bdw_ref[:, cols]
        for t in range(nt):
            acc = wk[0] * xp_ref[t, :, cols]
            for k in range(1, width):
                acc = acc + wk[k] * xp_ref[t + k, :, cols]
            y_ref[t * bs:(t + 1) * bs, cols] = acc + bias
    h = h_ref[...].reshape(nt * bs, d)
    out = _ln_swish_pw2(y_ref[...], h, lng_ref, lnb_ref, w2_ref, b2_ref)
    o_ref[...] = out.reshape(nt, bs, d)


def _a2_tm_call(xp, h, wdw, bdw, lng, lnb, w2, b2, bs):
    nt, db, d = h.shape
    width = wdw.shape[0]
    return pl.pallas_call(
        functools.partial(_a2_tm_kernel, width=width, cw=2 * LANES),
        grid=(db // bs,),
        in_specs=[pl.BlockSpec((nt + width - 1, bs, d), lambda i: (0, i, 0)),
                  pl.BlockSpec((nt, bs, d), lambda i: (0, i, 0)),
                  _const((width, d)), _const((1, d)), _const((1, d)), _const((1, d)), _const((d, d)), _const((1, d))],
        out_specs=pl.BlockSpec((nt, bs, d), lambda i: (0, i, 0)),
        out_shape=jax.ShapeDtypeStruct((nt, db, d), F32),
        scratch_shapes=[pltpu.VMEM((nt * bs, d), F32)],
        compiler_params=_cparams(("parallel",)),
        name="conv_dw_ln_pw2_sample",
    )(xp, h, wdw, bdw, lng, lnb, w2, b2)


def _ffn_kernel(*refs, mode, tm, fc, shift, ngroups):
    if mode == "prev":
        x_ref, xh_ref, prev_ref, g_ref, wup_ref, wdw_ref, wdn_ref, o_ref, tail_ref = refs
    elif mode == "zero":
        x_ref, g_ref, wup_ref, wdw_ref, wdn_ref, o_ref, tail_ref = refs
    else:
        x_ref, st_ref, g_ref, wup_ref, wdw_ref, wdn_ref, o_ref, tail_ref = refs
    x = x_ref[...]
    d = x.shape[-1]
    f = wdn_ref.shape[0]
    if mode == "prev":
        first = pl.program_id(1) == 0
        halo = jnp.where(first, prev_ref[...], xh_ref[...])
        xe = jnp.concatenate([halo, x], axis=0)
    elif mode == "zero":
        xe = jnp.concatenate([jnp.zeros((8, d), F32), x], axis=0)
    else:
        xe = x
    pre = xe.shape[0] - tm if mode != "state" else 2 * shift
    xn = (xe * _rms_scale(xe) * g_ref[...]).astype(BF16)
    ntail = tail_ref.shape[1]
    fg = wdn_ref.shape[0]
    acc = jnp.zeros((tm, d), F32)
    for j in range(fg // fc):
        cols = slice(j * fc, (j + 1) * fc)
        act = None
        for gu in range(2):
            hu = _dot(xn, wup_ref[gu, :, cols])
            tail_ref[gu, :, cols] = hu[hu.shape[0] - ntail:, :]
            if mode == "state":
                hu = jnp.concatenate([st_ref[gu, 0, :, cols], st_ref[gu, 1, :, cols], hu], axis=0)
            w = wdw_ref[gu, :, cols]
            conv = (w[0:1] * hu[pre - 2 * shift:pre - 2 * shift + tm]
                    + w[1:2] * hu[pre - shift:pre - shift + tm]
                    + w[2:3] * hu[pre:pre + tm])
            act = conv * _sigmoid(conv) if gu == 0 else act * conv
        acc = acc + _dot(act.astype(BF16), wdn_ref[cols, :])
    if ngroups == 1:
        o_ref[...] = x + acc
    else:
        gj = pl.program_id(2)

        @pl.when(gj == 0)
        def _():
            o_ref[...] = x + acc

        @pl.when(gj > 0)
        def _():
            o_ref[...] = o_ref[...] + acc


def _ffn_call(x, g, wup, wdw, wdn, tm, prev=None, state=None):
    bsz, t, d = x.shape
    f = wdn.shape[0]
    fc = FFN_CHUNK if f % FFN_CHUNK == 0 else LANES
    fg = fc if state is not None else f
    ngroups = f // fg
    row = pl.BlockSpec((None, tm, d), lambda bi, i, gj: (bi, i, 0))
    consts = [pl.BlockSpec((1, d), lambda bi, i, gj: (0, 0)),
              pl.BlockSpec((2, d, fg), lambda bi, i, gj: (0, 0, gj)),
              pl.BlockSpec((2, 3, fg), lambda bi, i, gj: (0, 0, gj)),
              pl.BlockSpec((fg, d), lambda bi, i, gj: (gj, 0))]
    if state is not None:
        assert bsz == 1 and t == tm
        mode, shift, ntail = "state", state.shape[2], 2 * state.shape[2]
        in_specs = [row, pl.BlockSpec(state.shape[:3] + (fg,), lambda bi, i, gj: (0, 0, 0, gj))] + consts
        args = (x, state, g, wup, wdw, wdn)
    elif prev is not None:
        mode, shift, ntail = "prev", 1, 8
        k = tm // 8
        in_specs = [row, pl.BlockSpec((None, 8, d), lambda bi, i, gj: (bi, jnp.maximum(i * k - 1, 0), 0)),
                    pl.BlockSpec((8, d), lambda bi, i, gj: (0, 0))] + consts
        args = (x, x, prev, g, wup, wdw, wdn)
    else:
        assert t == tm
        mode, shift, ntail = "zero", 1, 8
        in_specs = [row] + consts
        args = (x, g, wup, wdw, wdn)
    out, tail = pl.pallas_call(
        functools.partial(_ffn_kernel, mode=mode, tm=tm, fc=fc, shift=shift, ngroups=ngroups),
        grid=(bsz, t // tm, ngroups),
        in_specs=in_specs,
        out_specs=[row, pl.BlockSpec((None, 2, ntail, fg), lambda bi, i, gj: (bi, 0, 0, gj))],
        out_shape=[jax.ShapeDtypeStruct((bsz, t, d), F32), jax.ShapeDtypeStruct((bsz, 2, ntail, f), F32)],
        compiler_params=_cparams(("parallel", "arbitrary", "arbitrary")),
        name="conv_ffn_" + mode,
    )(*args)
    return out, tail


def _wo_kernel(h_ref, o_ref, w_ref, out_ref):
    out_ref[...] = h_ref[...] + _dot(o_ref[...], w_ref[...])


def _wo_call(h, o, w, tm):
    bsz, t, d = h.shape
    hv = o.shape[-1]
    return pl.pallas_call(
        _wo_kernel,
        grid=(bsz, t // tm),
        in_specs=[pl.BlockSpec((None, tm, d), lambda bi, i: (bi, i, 0)),
                  pl.BlockSpec((None, tm, hv), lambda bi, i: (bi, i, 0)), _const((hv, d))],
        out_specs=pl.BlockSpec((None, tm, d), lambda bi, i: (bi, i, 0)),
        out_shape=jax.ShapeDtypeStruct((bsz, t, d), F32),
        compiler_params=_cparams(("parallel", "parallel")),
        name="attn_out_proj",
    )(h, o, w)


def _proj_kernel(h_ref, cs_ref, sn_ref, gkv_ref, gmix_ref, wkv_ref, glat_ref, gkr_ref, wdq_ref, gql_ref,
                 wqm_ref, wqr_ref, gq_ref, wuk_ref, gk_ref, wuv_ref,
                 c_ref, kr_ref, q_ref, k_ref, v_ref, *, nheads, nope, rope, scale):
    h = h_ref[...]
    hs = h * _rms_scale(h)
    hn = (hs * gkv_ref[...]).astype(BF16)
    xq = (hs * gmix_ref[...]).astype(BF16)
    z = _dot(hn, wkv_ref[...])
    cpre, a, ar = z[:, :LANES], z[:, LANES:2 * LANES], z[:, 2 * LANES:3 * LANES]
    c = cpre * _rms_scale(cpre) * glat_ref[...]
    c_ref[...] = c
    cs = cs_ref[...]
    sn = sn_ref[...]
    ra = lax.rsqrt(jnp.sum(a * a, axis=-1, keepdims=True) * (1.0 / rope) + NORM_EPS)
    kr = ra * (a * gkr_ref[...] * cs + ar * sn)
    kr_ref[...] = kr
    cb = c.astype(BF16)
    v_ref[...] = _dot(cb, wuv_ref[...]).astype(BF16)
    kn = _dot(cb, wuk_ref[...])
    ql = _dot(xq, wdq_ref[...])
    ql = (ql * _rms_scale(ql) * gql_ref[...]).astype(BF16)
    qm = _dot(ql, wqm_ref[...])
    qr = _dot(ql, wqr_ref[...])
    lane = lax.broadcasted_iota(jnp.int32, (1, HEAD_SLOT), 1)
    mn = (lane < nope).astype(F32)
    mr = jnp.logical_and(lane >= nope, lane < nope + rope).astype(F32)
    gq = gq_ref[...]
    gk = gk_ref[...]
    for hd in range(nheads):
        sl = slice(hd * HEAD_SLOT, (hd + 1) * HEAD_SLOT)
        qh = qm[:, sl]
        sq = qh * qh
        rn = lax.rsqrt(jnp.sum(sq * mn, axis=-1, keepdims=True) * (1.0 / nope) + NORM_EPS)
        rr = lax.rsqrt(jnp.sum(sq * mr, axis=-1, keepdims=True) * (1.0 / rope) + NORM_EPS)
        qv = (qh * gq * cs + qr[:, sl] * sn) * (mn * rn + mr * rr) * scale
        q_ref[:, sl] = qv.astype(BF16)
        kh = kn[:, sl]
        rk = lax.rsqrt(jnp.sum(kh * kh, axis=-1, keepdims=True) * (1.0 / nope) + NORM_EPS)
        k_ref[:, sl] = (kh * rk * gk + kr).astype(BF16)


def _proj_call(h, cs, sn, w, tm, nheads, nope, rope, vdim):
    bsz, t, d = h.shape
    ql = w["wdq"].shape[1]
    hs = nheads * HEAD_SLOT
    row = lambda n: pl.BlockSpec((None, tm, n), lambda bi, i: (bi, i, 0))
    tab = pl.BlockSpec((tm, HEAD_SLOT), lambda bi, i: (i, 0))
    in_specs = [row(d), tab, tab, _const((1, d)), _const((1, d)), _const((d, 3 * LANES)), _const((1, LANES)),
                _const((1, HEAD_SLOT)), _const((d, ql)), _const((1, ql)), _const((ql, hs)), _const((ql, hs)),
                _const((1, HEAD_SLOT)), _const((LANES, hs)), _const((1, HEAD_SLOT)), _const((LANES, nheads * vdim))]
    return pl.pallas_call(
        functools.partial(_proj_kernel, nheads=nheads, nope=nope, rope=rope,
                          scale=LOG2E / math.sqrt(nope + rope)),
        grid=(bsz, t // tm),
        in_specs=in_specs,
        out_specs=[row(LANES), row(HEAD_SLOT), row(hs), row(hs), row(nheads * vdim)],
        out_shape=[jax.ShapeDtypeStruct((bsz, t, LANES), F32), jax.ShapeDtypeStruct((bsz, t, HEAD_SLOT), F32),
                   jax.ShapeDtypeStruct((bsz, t, hs), BF16), jax.ShapeDtypeStruct((bsz, t, hs), BF16),
                   jax.ShapeDtypeStruct((bsz, t, nheads * vdim), BF16)],
        compiler_params=_cparams(("parallel", "parallel")),
        name="mla_proj",
    )(h, cs, sn, w["gkv"], w["gmix"], w["wkv"], w["glat"], w["gkr"], w["wdq"], w["gql"], w["wqm"], w["wqr"],
      w["gq"], w["wuk"], w["gk"], w["wuv"])


def _attn_update(hd, r0, nr, q_ref, kb, vb, mask, m_ref, l_ref, acc_ref, first):
    rows = slice(r0, r0 + nr)
    q = q_ref[rows, hd * HEAD_SLOT:(hd + 1) * HEAD_SLOT]
    s = _dot_nt(q, kb)
    if mask is not None:
        s = jnp.where(mask, s, NEG)
    nk = s.shape[1]
    m_cur = jnp.max(s, axis=1, keepdims=True)
    if first:
        m_new = jnp.broadcast_to(m_cur, (nr, LANES))
    else:
        m_prev = m_ref[hd, rows, :]
        m_new = jnp.maximum(m_prev, m_cur)
    if nk % LANES == 0:
        mb = jnp.concatenate([m_new] * (nk // LANES), axis=1) if nk > LANES else m_new
    else:
        mb = m_new[:, :nk]
    p = jnp.exp2(s - mb)
    ps = jnp.sum(p, axis=1, keepdims=True)
    pv = _dot(p.astype(BF16), vb)
    if first:
        l_ref[hd, rows, :] = jnp.broadcast_to(ps, (nr, LANES))
        acc_ref[hd, rows, :] = pv
    else:
        alpha = jnp.exp2(m_prev - m_new)
        l_ref[hd, rows, :] = alpha * l_ref[hd, rows, :] + ps
        acc_ref[hd, rows, :] = alpha * acc_ref[hd, rows, :] + pv
    m_ref[hd, rows, :] = m_new


def _attn_kernel(*refs, tq, tk, dk, has_prefix, vdim, hps):
    if has_prefix:
        q_ref, k_ref, v_ref, pk_ref, pv_ref, o_ref, m_ref, l_ref, acc_ref = refs
    else:
        q_ref, k_ref, v_ref, o_ref, m_ref, l_ref, acc_ref = refs
    i = pl.program_id(2)
    kcols = lambda hd: slice(hd * HEAD_SLOT, (hd + 1) * HEAD_SLOT)
    vcols = lambda hd: slice((hd // 2) * LANES, (hd // 2 + 1) * LANES)
    if has_prefix:
        for hd in range(hps):
            _attn_update(hd, 0, tq, q_ref, pk_ref[:, kcols(hd)], pv_ref[:, vcols(hd)], None,
                         m_ref, l_ref, acc_ref, True)
        nfull = i * (tq // tk)

        def body(j, carry):
            k0 = pl.multiple_of(j * tk, tk)
            for hd in range(hps):
                _attn_update(hd, 0, tq, q_ref, k_ref[pl.ds(k0, tk), kcols(hd)], v_ref[pl.ds(k0, tk), vcols(hd)],
                             None, m_ref, l_ref, acc_ref, False)
            return carry

        lax.fori_loop(0, nfull, body, 0)
    for jj in range(tq // dk):
        r0 = jj * dk
        nr = tq - r0
        k0 = pl.multiple_of(i * tq + r0, dk)
        row = lax.broadcasted_iota(jnp.int32, (nr, dk), 0)
        col = lax.broadcasted_iota(jnp.int32, (nr, dk), 1)
        mask = col <= row
        for hd in range(hps):
            _attn_update(hd, r0, nr, q_ref, k_ref[pl.ds(k0, dk), kcols(hd)], v_ref[pl.ds(k0, dk), vcols(hd)], mask,
                         m_ref, l_ref, acc_ref, (not has_prefix) and jj == 0)
    lane = lax.broadcasted_iota(jnp.int32, (tq, LANES), 1)
    for pr in range(hps // 2):
        o0 = acc_ref[2 * pr] / l_ref[2 * pr]
        o1 = acc_ref[2 * pr + 1] / l_ref[2 * pr + 1]
        o_ref[:, pr * LANES:(pr + 1) * LANES] = jnp.where(lane < vdim, o0, o1).astype(o_ref.dtype)


def _attn_call(q, k, v, pk, pv, tq, tk, dk, nheads, vdim):
    bsz, t, _ = q.shape
    hps = ATTN_HEADS_PER_STEP if nheads % ATTN_HEADS_PER_STEP == 0 else 2
    assert 2 * vdim == LANES and nheads % hps == 0 and t % tq == 0 and tq % tk == 0 and tq % dk == 0
    has_prefix = pk is not None
    assert has_prefix or t == tq
    in_specs = [pl.BlockSpec((None, tq, hps * HEAD_SLOT), lambda bi, hg, i: (bi, i, hg)),
                pl.BlockSpec((None, t, hps * HEAD_SLOT), lambda bi, hg, i: (bi, 0, hg)),
                pl.BlockSpec((None, t, hps * vdim), lambda bi, hg, i: (bi, 0, hg))]
    args = [q, k, v]
    if has_prefix:
        npre = pk.shape[0]
        in_specs += [pl.BlockSpec((npre, hps * HEAD_SLOT), lambda bi, hg, i: (0, hg)),
                     pl.BlockSpec((npre, hps * vdim), lambda bi, hg, i: (0, hg))]
        args += [pk, pv]
    return pl.pallas_call(
        functools.partial(_attn_kernel, tq=tq, tk=tk, dk=dk, has_prefix=has_prefix, vdim=vdim, hps=hps),
        grid=(bsz, nheads // hps, t // tq),
        in_specs=in_specs,
        out_specs=pl.BlockSpec((None, tq, hps * vdim), lambda bi, hg, i: (bi, i, hg)),
        out_shape=jax.ShapeDtypeStruct((bsz, t, nheads * vdim), BF16),
        scratch_shapes=[pltpu.VMEM((hps, tq, LANES), F32)] * 3,
        compiler_params=_cparams(("parallel", "parallel", "arbitrary")),
        name="mla_prompt_attn",
    )(*args)


def _qt_kernel(q_ref, w_ref, o_ref, *, nheads):
    for hd in range(nheads):
        o_ref[hd] = _dot(q_ref[:, hd * HEAD_SLOT:(hd + 1) * HEAD_SLOT], w_ref[hd]).astype(BF16)


def _qt_call(q, w, nheads):
    r = q.shape[0]
    return pl.pallas_call(
        functools.partial(_qt_kernel, nheads=nheads),
        out_shape=jax.ShapeDtypeStruct((nheads, r, LANES), BF16),
        compiler_params=pltpu.CompilerParams(vmem_limit_bytes=VMEM_LIMIT),
        name="mla_absorb_q",
    )(q, w)


def _uv_kernel(ol_ref, w_ref, o_ref, *, nheads):
    for hp in range(nheads // 2):
        r = (_dot(ol_ref[2 * hp].astype(BF16), w_ref[2 * hp])
             + _dot(ol_ref[2 * hp + 1].astype(BF16), w_ref[2 * hp + 1]))
        o_ref[:, hp * LANES:(hp + 1) * LANES] = r.astype(BF16)


def _uv_call(ol, w, nheads, vdim):
    r = ol.shape[1]
    return pl.pallas_call(
        functools.partial(_uv_kernel, nheads=nheads),
        out_shape=jax.ShapeDtypeStruct((r, nheads * vdim), BF16),
        compiler_params=pltpu.CompilerParams(vmem_limit_bytes=VMEM_LIMIT),
        name="mla_value_up",
    )(ol, w)


def _paged_scores(wukt_ref, qt_ref, qr_ref, cb, krt, nheads, nq, nope):
    n = cb.shape[0]
    kt = _dot_nt(wukt_ref[...], cb)
    ss = jnp.sum((kt * kt).reshape(nheads, nope, n), axis=1)
    rs = lax.rsqrt(ss * (1.0 / nope) + NORM_EPS)
    sraw = _dot_nt(qt_ref[...], cb)
    srope = _dot(qr_ref[...], krt)
    return (sraw.reshape(nq, nheads, n) * rs[None, :, :]).reshape(nq * nheads, n) + srope


def _paged_kernel(pt_ref, lat_hbm, krt_hbm, wukt_ref, qt_ref, qr_ref, cn_ref, krn_ref, o_ref,
                  latbuf, krbuf, sem, m_ref, l_ref, acc_ref, *, npages, ch, sub, nheads, nq, nope):
    b = pl.program_id(0)
    nb = pl.num_programs(0)
    nc = npages // ch
    page = latbuf.shape[1] // ch
    nrow = nq * nheads

    def lat_copy(pid, slot, p):
        return pltpu.make_async_copy(lat_hbm.at[pid], latbuf.at[slot, pl.ds(p * page, page), :], sem.at[slot, 0])

    def krt_copy(pid, slot, p):
        return pltpu.make_async_copy(krt_hbm.at[pid], krbuf.at[slot, :, pl.ds(p * page, page)], sem.at[slot, 1])

    def start_chunk(bb, c, slot):
        for p in range(ch):
            pid = pt_ref[bb * npages + c * ch + p]
            lat_copy(pid, slot, p).start()
            krt_copy(pid, slot, p).start()

    def wait_chunk(slot):
        for p in range(ch):
            lat_copy(0, slot, p).wait()
            krt_copy(0, slot, p).wait()

    def update(s, cb):
        n = s.shape[1]
        m_prev = m_ref[...]
        m_new = jnp.maximum(m_prev, jnp.max(s, axis=1, keepdims=True))
        p = jnp.exp2(s - jnp.concatenate([m_new] * (n // LANES), axis=1))
        alpha = jnp.exp2(m_prev - m_new)
        l_ref[...] = alpha * l_ref[...] + jnp.sum(p, axis=1, keepdims=True)
        acc_ref[...] = alpha * acc_ref[...] + _dot(p.astype(BF16), cb)
        m_ref[...] = m_new

    def compute(slot):
        for sb in range(ch * page // sub):
            keys = slice(sb * sub, (sb + 1) * sub)
            cb = latbuf[slot, keys, :].astype(BF16)
            krt = krbuf[slot, :, keys].astype(BF16)
            update(_paged_scores(wukt_ref, qt_ref, qr_ref, cb, krt, nheads, nq, nope), cb)

    @pl.when(b == 0)
    def _():
        start_chunk(0, 0, 0)

    cb = cn_ref[...].astype(BF16)
    s = _paged_scores(wukt_ref, qt_ref, qr_ref, cb, krn_ref[...].astype(BF16), nheads, nq, nope)
    n = s.shape[1]
    row = lax.broadcasted_iota(jnp.int32, (nrow, n), 0)
    col = lax.broadcasted_iota(jnp.int32, (nrow, n), 1)
    s = jnp.where(col * nheads <= row, s, NEG)
    m = jnp.max(s, axis=1, keepdims=True)
    p = jnp.exp2(s - m)
    m_ref[...] = jnp.broadcast_to(m, (nrow, LANES))
    l_ref[...] = jnp.broadcast_to(jnp.sum(p, axis=1, keepdims=True), (nrow, LANES))
    acc_ref[...] = _dot(p.astype(BF16), cb)

    def pair_body(cp, carry):
        c0 = 2 * cp
        start_chunk(b, c0 + 1, 1)
        wait_chunk(0)
        compute(0)

        @pl.when(c0 + 2 < nc)
        def _():
            start_chunk(b, c0 + 2, 0)

        @pl.when(jnp.logical_and(c0 + 2 == nc, b + 1 < nb))
        def _():
            start_chunk(b + 1, 0, 0)

        wait_chunk(1)
        compute(1)
        return carry

    lax.fori_loop(0, nc // 2, pair_body, 0)
    o_ref[...] = acc_ref[...] / l_ref[...]


def _paged_call(page_table, cache_lat, cache_krt, wukt, qt, qr, cn, krn, nheads, nq, nope):
    db, npages = page_table.shape
    _, page, clat = cache_lat.shape
    rdim = cache_krt.shape[1]
    ch = PAGES_PER_CHUNK
    while npages % (2 * ch):
        ch //= 2
    sub = min(PAGED_SUB_KEYS, ch * page)
    assert ch >= 1 and (ch * page) % sub == 0
    nrow = nq * nheads
    per_b = lambda shape: pl.BlockSpec((None,) + shape, lambda b, pt: (b, 0, 0))
    in_specs = [pl.BlockSpec(memory_space=pl.ANY), pl.BlockSpec(memory_space=pl.ANY),
                pl.BlockSpec(wukt.shape, lambda b, pt: (0, 0)),
                per_b((nrow, clat)), per_b((nrow, rdim)), per_b(cn.shape[1:]), per_b(krn.shape[1:])]
    return pl.pallas_call(
        functools.partial(_paged_kernel, npages=npages, ch=ch, sub=sub, nheads=nheads, nq=nq, nope=nope),
        grid_spec=pltpu.PrefetchScalarGridSpec(
            num_scalar_prefetch=1,
            grid=(db,),
            in_specs=in_specs,
            out_specs=per_b((nrow, clat)),
            scratch_shapes=[pltpu.VMEM((2, ch * page, clat), F32), pltpu.VMEM((2, rdim, ch * page), F32),
                            pltpu.SemaphoreType.DMA((2, 2))] + [pltpu.VMEM((nrow, LANES), F32)] * 3),
        out_shape=jax.ShapeDtypeStruct((db, nrow, clat), F32),
        compiler_params=_cparams(("arbitrary",)),
        name="mla_paged_attn",
    )(page_table.reshape(-1), cache_lat, cache_krt, wukt, qt, qr, cn, krn)


def _slot(x, lo, width=HEAD_SLOT):
    n = x.shape[-1]
    pad = [(0, 0)] * (x.ndim - 1) + [(lo, width - lo - n)]
    return jnp.pad(x, pad)


def _half_rot_cols(w):
    half = w.shape[-1] // 2
    return jnp.concatenate([-w[..., half:], w[..., :half]], axis=-1)


def _rope_tables(pos, rope, nope):
    half = rope // 2
    inv = ROPE_THETA ** (-jnp.arange(half, dtype=F32) / half)
    ang = pos.astype(F32)[:, None] * inv[None, :]
    cos, sin = jnp.cos(ang), jnp.sin(ang)
    cs = jnp.concatenate([jnp.ones((pos.shape[0], nope), F32), cos, cos], axis=1)
    sn = jnp.concatenate([jnp.zeros((pos.shape[0], nope), F32), sin, sin], axis=1)
    return _slot(cs, 0), _slot(sn, 0)


def kernel(x_prompt, x_sample, state_conv_a, state_ffn_conv, cache_kv_latent, cache_k_rope, page_table, meta_tokens, norm_mix, norm_ffn, a_w_pw1, a_b_pw1, a_w_dw, a_b_dw, a_ln_g, a_ln_b, a_w_pw2, a_b_pw2, ffn_w_up, ffn_w_dw, ffn_w_down, kv_norm, mla_w_dkv, mla_lat_norm, mla_w_kr, mla_knorm_rope, mla_w_uk, mla_w_uv, mla_knorm_nope, mla_w_dq, mla_q_lat_norm, mla_w_uq, mla_qnorm_nope, mla_qnorm_rope, mla_w_o):
    bsz, seq, d = x_prompt.shape
    db, nq, _ = x_sample.shape
    nmeta = meta_tokens.shape[0]
    depth = norm_mix.shape[0]
    n_a = a_w_pw1.shape[0]
    n_b = mla_w_dq.shape[0]
    f = ffn_w_down.shape[1]
    clat, nheads, nope = mla_w_uk.shape
    rope = mla_w_kr.shape[1]
    vdim = mla_w_uv.shape[2]
    width_a = a_w_dw.shape[1]
    npages, page = page_table.shape[1], cache_kv_latent.shape[1]
    past_len = npages * page
    assert n_b == 1 and depth == n_a + n_b and clat == LANES and nope + rope <= HEAD_SLOT
    assert seq >= width_a - 1 and nmeta % 16 == 0 and nmeta >= 8
    tm = ROW_TILE if seq % ROW_TILE == 0 else seq
    rs = nq * db
    row2 = lambda v: v.reshape(1, -1)

    hp = x_prompt
    hm = meta_tokens.astype(F32)[None]
    hs = jnp.transpose(x_sample, (1, 0, 2)).reshape(1, rs, d)

    conv_a_p, conv_a_s, ffn_p, ffn_s = [], [], [], []

    def run_ffn(layer, hp, hm, hs):
        g = row2(norm_ffn[layer])
        wup = jnp.transpose(ffn_w_up[layer].reshape(d, 2, f), (1, 0, 2)).astype(BF16)
        wdw = jnp.transpose(ffn_w_dw[layer].reshape(3, 2, f), (1, 0, 2))
        wdn = ffn_w_down[layer].astype(BF16)
        st = jnp.transpose(state_ffn_conv[layer].reshape(db, 2, 2, f), (2, 1, 0, 3))
        hm_new, _ = _ffn_call(hm, g, wup, wdw, wdn, nmeta)
        hp_new, tail_p = _ffn_call(hp, g, wup, wdw, wdn, tm, prev=hm[0, nmeta - 8:])
        hs_new, tail_s = _ffn_call(hs, g, wup, wdw, wdn, rs, state=st)
        ffn_p.append(jnp.transpose(tail_p[:, :, 6:8, :], (0, 2, 1, 3)).reshape(bsz, 2, 2 * f))
        ffn_s.append(jnp.transpose(tail_s[0].reshape(2, 2, db, f), (2, 1, 0, 3)).reshape(db, 2, 2 * f))
        return hp_new, hm_new, hs_new

    for layer in range(n_a):
        g = row2(norm_mix[layer])
        w1 = a_w_pw1[layer].astype(BF16)
        b1 = row2(a_b_pw1[layer])
        conv_w = (a_w_dw[layer], row2(a_b_dw[layer]), row2(a_ln_g[layer]), row2(a_ln_b[layer]),
                  a_w_pw2[layer].astype(BF16), row2(a_b_pw2[layer]))
        um = _a1_call(hm, g, w1, b1, nmeta)
        up = _a1_call(hp, g, w1, b1, tm)
        us = _a1_call(hs, g, w1, b1, rs)
        halo = jnp.pad(um[0], ((32 - nmeta, 0), (0, 0))) if nmeta < 32 else um[0, nmeta - 32:]
        hm_new = _a2_seq_call(um, hm, None, *conv_w, nmeta)
        hp = _a2_seq_call(up, hp, halo, *conv_w, tm)
        hm = hm_new
        xp_s = jnp.concatenate([jnp.transpose(state_conv_a[layer], (1, 0, 2)), us.reshape(nq, db, d)], axis=0)
        hs = _a2_tm_call(xp_s, hs.reshape(nq, db, d), *conv_w, 32 if db % 32 == 0 else db).reshape(1, rs, d)
        conv_a_p.append(up[:, seq - (width_a - 1):])
        conv_a_s.append(jnp.transpose(xp_s[nq:], (1, 0, 2)))
        hp, hm, hs = run_ffn(layer, hp, hm, hs)

    layer = n_a
    g_kr = mla_knorm_rope
    wkr_g = mla_w_kr * g_kr[None, :]
    uq = mla_w_uq[0]
    uq_rope_g = uq[:, :, nope:] * mla_qnorm_rope[0][None, None, :]
    pw = {
        "gkv": row2(kv_norm), "gmix": row2(norm_mix[layer]),
        "wkv": jnp.concatenate([mla_w_dkv, _slot(mla_w_kr, nope), _slot(_half_rot_cols(wkr_g), nope)], axis=1).astype(BF16),
        "glat": row2(mla_lat_norm), "gkr": _slot(row2(g_kr), nope),
        "wdq": mla_w_dq[0].astype(BF16), "gql": row2(mla_q_lat_norm[0]),
        "wqm": _slot(uq, 0).reshape(uq.shape[0], nheads * HEAD_SLOT).astype(BF16),
        "wqr": _slot(_half_rot_cols(uq_rope_g), nope).reshape(uq.shape[0], nheads * HEAD_SLOT).astype(BF16),
        "gq": _slot(row2(jnp.concatenate([mla_qnorm_nope[0], mla_qnorm_rope[0]])), 0),
        "wuk": _slot(mla_w_uk, 0).reshape(clat, nheads * HEAD_SLOT).astype(BF16),
        "gk": _slot(row2(mla_knorm_nope), 0),
        "wuv": mla_w_uv.reshape(clat, nheads * vdim).astype(BF16),
    }
    cs_m, sn_m = _rope_tables(jnp.arange(nmeta), rope, nope)
    cs_p, sn_p = _rope_tables(nmeta + jnp.arange(seq), rope, nope)
    cs_s, sn_s = _rope_tables(jnp.repeat(past_len + jnp.arange(nq), db), rope, nope)
    proj = functools.partial(_proj_call, w=pw, nheads=nheads, nope=nope, rope=rope, vdim=vdim)
    c_m, kr_m, q_m, k_m, v_m = proj(hm, cs_m, sn_m, tm=nmeta)
    c_p, kr_p, q_p, k_p, v_p = proj(hp, cs_p, sn_p, tm=tm)
    c_s, kr_s, q_s, _, _ = proj(hs, cs_s, sn_s, tm=rs)
    kr_m, kr_p, kr_s = (x[..., nope:nope + rope] for x in (kr_m, kr_p, kr_s))

    tq = ATTN_Q_TILE if seq % ATTN_Q_TILE == 0 else seq
    tk = ATTN_K_TILE if tq % ATTN_K_TILE == 0 else tq
    dk = ATTN_DIAG_TILE if tq % ATTN_DIAG_TILE == 0 else tq
    o_m = _attn_call(q_m, k_m, v_m, None, None, nmeta, nmeta, nmeta, nheads, vdim)
    o_p = _attn_call(q_p, k_p, v_p, k_m[0], v_m[0], tq, tk, dk, nheads, vdim)

    wukt = jnp.transpose(mla_w_uk.reshape(clat, nheads * nope)).astype(BF16)
    wqt = jnp.transpose(mla_w_uk, (1, 2, 0)) * mla_knorm_nope[None, :, None]
    wqt = jnp.pad(wqt, ((0, 0), (0, HEAD_SLOT - nope), (0, 0))).astype(BF16)
    qt = _qt_call(q_s[0], wqt, nheads)
    qt = jnp.transpose(qt.reshape(nheads, nq, db, clat), (2, 1, 0, 3)).reshape(db, nq * nheads, clat)
    qr = jnp.transpose(q_s[0].reshape(nq, db, nheads, HEAD_SLOT)[..., nope:nope + rope], (1, 0, 2, 3))
    qr = qr.reshape(db, nq * nheads, rope)
    nnew = LANES
    cn = jnp.pad(jnp.transpose(c_s[0].reshape(nq, db, clat), (1, 0, 2)), ((0, 0), (0, nnew - nq), (0, 0)))
    krn = jnp.pad(jnp.transpose(kr_s[0].reshape(nq, db, rope), (1, 2, 0)), ((0, 0), (0, 0), (0, nnew - nq)))
    cache_krt = jnp.transpose(cache_k_rope, (0, 2, 1))
    o_lat = _paged_call(page_table, cache_kv_latent, cache_krt, wukt, qt, qr, cn, krn, nheads, nq, nope)
    ol = jnp.transpose(o_lat.reshape(db, nq, nheads, clat), (2, 1, 0, 3)).reshape(nheads, rs, clat)
    wuv_pad = jnp.stack([_slot(mla_w_uv[:, hd, :], (hd % 2) * vdim, LANES) for hd in range(nheads)]).astype(BF16)
    o_s = _uv_call(ol, wuv_pad, nheads, vdim)[None]

    wo = mla_w_o[0].astype(BF16)
    hm = _wo_call(hm, o_m, wo, nmeta)
    hp = _wo_call(hp, o_p, wo, tm)
    hs = _wo_call(hs, o_s, wo, rs)
    hp, hm, hs = run_ffn(layer, hp, hm, hs)

    y_sample = jnp.transpose(hs.reshape(nq, db, d), (1, 0, 2))
    kv_lat_p = jnp.concatenate([jnp.broadcast_to(c_m, (bsz, nmeta, clat)), c_p], axis=1)
    k_rope_p = jnp.concatenate([jnp.broadcast_to(kr_m, (bsz, nmeta, rope)), kr_p], axis=1)
    kv_lat_s = jnp.transpose(c_s[0].reshape(nq, db, clat), (1, 0, 2))
    k_rope_s = jnp.transpose(kr_s[0].reshape(nq, db, rope), (1, 0, 2))
    return (hp, y_sample, jnp.stack(conv_a_p), jnp.stack(conv_a_s), jnp.stack(ffn_p), jnp.stack(ffn_s),
            kv_lat_p, k_rope_p, kv_lat_s, k_rope_s)
```

```python
import functools
import math

import jax
import jax.numpy as jnp
from jax import lax
from jax.experimental import pallas as pl
from jax.experimental.pallas import tpu as pltpu

F32 = jnp.float32
BF16 = jnp.bfloat16
NORM_EPS = 1e-6
ROPE_THETA = 10000.0
NEG = -1e30

LANES = 128
HEAD_SLOT = 128
ROW_TILE = 512
ATTN_Q_TILE = 1024
ATTN_K_TILE = 512
ATTN_DIAG_TILE = 256
FFN_CHUNK = 256
FFN_SEQ_CHUNK = 2816
ATTN_HEADS_PER_STEP = 4
PAGES_PER_CHUNK = 32
PAGED_SUB_KEYS = 1024
LOG2E = 1.4426950408889634
VMEM_LIMIT = 56 * 1024 * 1024


def _cparams(sem):
    return pltpu.CompilerParams(dimension_semantics=sem, vmem_limit_bytes=VMEM_LIMIT)


def _const(shape):
    zeros = (0,) * len(shape)
    return pl.BlockSpec(shape, lambda *_: zeros)


def _sigmoid(x):
    return 1.0 / (1.0 + jnp.exp(-x))


def _rms_scale(x):
    return lax.rsqrt(jnp.mean(x * x, axis=-1, keepdims=True) + NORM_EPS)


def _dot(a, b):
    return jnp.dot(a, b, preferred_element_type=F32)


def _dot_nt(a, b):
    return lax.dot_general(a, b, (((1,), (1,)), ((), ())), preferred_element_type=F32)


def _a1_kernel(x_ref, g_ref, w_ref, b_ref, u_ref):
    x = x_ref[...]
    xn = x * _rms_scale(x) * g_ref[...]
    z = _dot(xn.astype(BF16), w_ref[...]) + b_ref[...]
    d = u_ref.shape[-1]
    u_ref[...] = z[:, :d] * _sigmoid(z[:, d:])


def _a1_call(x, g, w, b, tm):
    bsz, t, d = x.shape
    return pl.pallas_call(
        _a1_kernel,
        grid=(bsz, t // tm),
        in_specs=[pl.BlockSpec((None, tm, d), lambda bi, i: (bi, i, 0)),
                  _const((1, d)), _const((d, 2 * d)), _const((1, 2 * d))],
        out_specs=pl.BlockSpec((None, tm, d), lambda bi, i: (bi, i, 0)),
        out_shape=jax.ShapeDtypeStruct((bsz, t, d), F32),
        compiler_params=_cparams(("parallel", "parallel")),
        name="conv_pw1_glu",
    )(x, g, w, b)


def _ln_swish_pw2(y, h, lng_ref, lnb_ref, w2_ref, b2_ref):
    mu = jnp.mean(y, axis=-1, keepdims=True)
    yc = y - mu
    var = jnp.mean(yc * yc, axis=-1, keepdims=True)
    yn = yc * lax.rsqrt(var + NORM_EPS) * lng_ref[...] + lnb_ref[...]
    a = yn * _sigmoid(yn)
    return h + _dot(a.astype(BF16), w2_ref[...]) + b2_ref[...]


def _a2_seq_kernel(*refs, tm, halo, width, has_prev):
    if has_prev:
        (u_ref, uh_ref, prev_ref, h_ref, wdw_ref, bdw_ref, lng_ref, lnb_ref, w2_ref, b2_ref,
         o_ref, ext_ref, y_ref) = refs
        i = pl.program_id(1)

        @pl.when(i == 0)
        def _():
            ext_ref[0:halo, :] = prev_ref[...]

        @pl.when(i > 0)
        def _():
            ext_ref[0:halo, :] = uh_ref[...]
    else:
        (u_ref, h_ref, wdw_ref, bdw_ref, lng_ref, lnb_ref, w2_ref, b2_ref,
         o_ref, ext_ref, y_ref) = refs
        ext_ref[0:halo, :] = jnp.zeros((halo, ext_ref.shape[1]), F32)
    ext_ref[halo:halo + tm, :] = u_ref[...]
    d = u_ref.shape[-1]
    off = halo - (width - 1)

    taps = {}
    for k in range(width):
        taps.setdefault((off + k) % 8, []).append(((off + k) // 8, k))
    nblk = tm // 8

    def col_body(c, carry):
        c0 = pl.multiple_of(c * LANES, LANES)
        cols = pl.ds(c0, LANES)
        bias = jnp.broadcast_to(bdw_ref[:, cols], (8, LANES))
        sub = lax.broadcasted_iota(jnp.int32, (8, LANES), 0)
        xs = {}

        def xblk(jb):
            if jb not in xs:
                xs[jb] = ext_ref[pl.ds(jb * 8, 8), cols]
            return xs[jb]

        def zsum(r, jb):
            acc = None
            for a, k in taps[r]:
                term = wdw_ref[pl.ds(k * 8, 8), cols] * xblk(jb + a)
                acc = term if acc is None else acc + term
            return acc

        zprev = {r: zsum(r, 0) for r in taps if r != 0}
        for jb in range(nblk):
            y = zsum(0, jb) + bias if 0 in taps else bias
            znext = {}
            for r in zprev:
                znext[r] = zsum(r, jb + 1)
                y = y + pltpu.roll(jnp.where(sub >= r, zprev[r], znext[r]), 8 - r, axis=0)
            y_ref[pl.ds(jb * 8, 8), cols] = y
            zprev = znext
        return carry

    lax.fori_loop(0, d // LANES, col_body, 0)
    o_ref[...] = _ln_swish_pw2(y_ref[...], h_ref[...], lng_ref, lnb_ref, w2_ref, b2_ref)


def _a2_seq_call(u, h, prev, wdw, bdw, lng, lnb, w2, b2, tm):
    bsz, t, d = u.shape
    width = wdw.shape[0]
    halo = 32
    assert halo >= width - 1 and halo % 8 == 0 and (tm % halo == 0 or prev is None)
    wdw = jnp.repeat(wdw, 8, axis=0)
    row = pl.BlockSpec((None, tm, d), lambda bi, i: (bi, i, 0))
    consts = [_const((width * 8, d)), _const((1, d)), _const((1, d)), _const((1, d)), _const((d, d)), _const((1, d))]
    if prev is not None:
        k = tm // halo
        in_specs = [row, pl.BlockSpec((None, halo, d), lambda bi, i: (bi, jnp.maximum(i * k - 1, 0), 0)),
                    _const((halo, d)), row] + consts
        args = (u, u, prev, h, wdw, bdw, lng, lnb, w2, b2)
    else:
        assert t == tm
        in_specs = [row, row] + consts
        args = (u, h, wdw, bdw, lng, lnb, w2, b2)
    return pl.pallas_call(
        functools.partial(_a2_seq_kernel, tm=tm, halo=halo, width=width, has_prev=prev is not None),
        grid=(bsz, t // tm),
        in_specs=in_specs,
        out_specs=row,
        out_shape=jax.ShapeDtypeStruct((bsz, t, d), F32),
        scratch_shapes=[pltpu.VMEM((halo + tm, d), F32), pltpu.VMEM((tm, d), F32)],
        compiler_params=_cparams(("parallel", "parallel")),
        name="conv_dw_ln_pw2",
    )(*args)


def _a2_tm_kernel(xp_ref, h_ref, wdw_ref, bdw_ref, lng_ref, lnb_ref, w2_ref, b2_ref, o_ref, y_ref, *, width, cw):
    nt, bs, d = h_ref.shape
    for c in range(d // cw):
        cols = slice(c * cw, (c + 1) * cw)
        wk = [jnp.broadcast_to(wdw_ref[k:k + 1, cols], (bs, cw)) for k in range(width)]
        bias = bdw_ref[:, cols]
        for t in range(nt):
            acc = wk[0] * xp_ref[t, :, cols]
            for k in range(1, width):
                acc = acc + wk[k] * xp_ref[t + k, :, cols]
            y_ref[t * bs:(t + 1) * bs, cols] = acc + bias
    h = h_ref[...].reshape(nt * bs, d)
    out = _ln_swish_pw2(y_ref[...], h, lng_ref, lnb_ref, w2_ref, b2_ref)
    o_ref[...] = out.reshape(nt, bs, d)


def _a2_tm_call(xp, h, wdw, bdw, lng, lnb, w2, b2, bs):
    nt, db, d = h.shape
    width = wdw.shape[0]
    return pl.pallas_call(
        functools.partial(_a2_tm_kernel, width=width, cw=2 * LANES),
        grid=(db // bs,),
        in_specs=[pl.BlockSpec((nt + width - 1, bs, d), lambda i: (0, i, 0)),
                  pl.BlockSpec((nt, bs, d), lambda i: (0, i, 0)),
                  _const((width, d)), _const((1, d)), _const((1, d)), _const((1, d)), _const((d, d)), _const((1, d))],
        out_specs=pl.BlockSpec((nt, bs, d), lambda i: (0, i, 0)),
        out_shape=jax.ShapeDtypeStruct((nt, db, d), F32),
        scratch_shapes=[pltpu.VMEM((nt * bs, d), F32)],
        compiler_params=_cparams(("parallel",)),
        name="conv_dw_ln_pw2_sample",
    )(xp, h, wdw, bdw, lng, lnb, w2, b2)


def _ffn_kernel(*refs, mode, tm, fc, shift, ngroups):
    if mode == "prev":
        x_ref, xh_ref, prev_ref, g_ref, wup_ref, wdw_ref, wdn_ref, o_ref, tail_ref = refs
    elif mode == "zero":
        x_ref, g_ref, wup_ref, wdw_ref, wdn_ref, o_ref, tail_ref = refs
    else:
        x_ref, st_ref, g_ref, wup_ref, wdw_ref, wdn_ref, o_ref, tail_ref = refs
    x = x_ref[...]
    d = x.shape[-1]
    f = wdn_ref.shape[0]
    if mode == "prev":
        first = pl.program_id(1) == 0
        halo = jnp.where(first, prev_ref[...], xh_ref[...])
        xe = jnp.concatenate([halo, x], axis=0)
    elif mode == "zero":
        xe = jnp.concatenate([jnp.zeros((8, d), F32), x], axis=0)
    else:
        xe = x
    pre = xe.shape[0] - tm if mode != "state" else 2 * shift
    xn = (xe * _rms_scale(xe) * g_ref[...]).astype(BF16)
    ntail = tail_ref.shape[1]
    fg = wdn_ref.shape[0]
    acc = jnp.zeros((tm, d), F32)
    for j in range(fg // fc):
        cols = slice(j * fc, (j + 1) * fc)
        act = None
        for gu in range(2):
            hu = _dot(xn, wup_ref[gu, :, cols])
            tail_ref[gu, :, cols] = hu[hu.shape[0] - ntail:, :]
            if mode == "state":
                hu = jnp.concatenate([st_ref[gu, 0, :, cols], st_ref[gu, 1, :, cols], hu], axis=0)
            w = wdw_ref[gu, :, cols]
            conv = (w[0:1] * hu[pre - 2 * shift:pre - 2 * shift + tm]
                    + w[1:2] * hu[pre - shift:pre - shift + tm]
                    + w[2:3] * hu[pre:pre + tm])
            act = conv * _sigmoid(conv) if gu == 0 else act * conv
        acc = acc + _dot(act.astype(BF16), wdn_ref[cols, :])
    if ngroups == 1:
        o_ref[...] = x + acc
    else:
        gj = pl.program_id(2)

        @pl.when(gj == 0)
        def _():
            o_ref[...] = x + acc

        @pl.when(gj > 0)
        def _():
            o_ref[...] = o_ref[...] + acc


def _ffn_call(x, g, wup, wdw, wdn, tm, prev=None, state=None):
    bsz, t, d = x.shape
    f = wdn.shape[0]
    fc = FFN_CHUNK if f % FFN_CHUNK == 0 else LANES
    if state is None and f % FFN_SEQ_CHUNK == 0:
        fc = FFN_SEQ_CHUNK
    fg = fc if state is not None else f
    ngroups = f // fg
    row = pl.BlockSpec((None, tm, d), lambda bi, i, gj: (bi, i, 0))
    consts = [pl.BlockSpec((1, d), lambda bi, i, gj: (0, 0)),
              pl.BlockSpec((2, d, fg), lambda bi, i, gj: (0, 0, gj)),
              pl.BlockSpec((2, 3, fg), lambda bi, i, gj: (0, 0, gj)),
              pl.BlockSpec((fg, d), lambda bi, i, gj: (gj, 0))]
    if state is not None:
        assert bsz == 1 and t == tm
        mode, shift, ntail = "state", state.shape[2], 2 * state.shape[2]
        in_specs = [row, pl.BlockSpec(state.shape[:3] + (fg,), lambda bi, i, gj: (0, 0, 0, gj))] + consts
        args = (x, state, g, wup, wdw, wdn)
    elif prev is not None:
        mode, shift, ntail = "prev", 1, 8
        k = tm // 8
        in_specs = [row, pl.BlockSpec((None, 8, d), lambda bi, i, gj: (bi, jnp.maximum(i * k - 1, 0), 0)),
                    pl.BlockSpec((8, d), lambda bi, i, gj: (0, 0))] + consts
        args = (x, x, prev, g, wup, wdw, wdn)
    else:
        assert t == tm
        mode, shift, ntail = "zero", 1, 8
        in_specs = [row] + consts
        args = (x, g, wup, wdw, wdn)
    out, tail = pl.pallas_call(
        functools.partial(_ffn_kernel, mode=mode, tm=tm, fc=fc, shift=shift, ngroups=ngroups),
        grid=(bsz, t // tm, ngroups),
        in_specs=in_specs,
        out_specs=[row, pl.BlockSpec((None, 2, ntail, fg), lambda bi, i, gj: (bi, 0, 0, gj))],
        out_shape=[jax.ShapeDtypeStruct((bsz, t, d), F32), jax.ShapeDtypeStruct((bsz, 2, ntail, f), F32)],
        compiler_params=_cparams(("parallel", "arbitrary", "arbitrary")),
        name="conv_ffn_" + mode,
    )(*args)
    return out, tail


def _wo_kernel(h_ref, o_ref, w_ref, out_ref):
    out_ref[...] = h_ref[...] + _dot(o_ref[...], w_ref[...])


def _wo_call(h, o, w, tm):
    bsz, t, d = h.shape
    hv = o.shape[-1]
    return pl.pallas_call(
        _wo_kernel,
        grid=(bsz, t // tm),
        in_specs=[pl.BlockSpec((None, tm, d), lambda bi, i: (bi, i, 0)),
                  pl.BlockSpec((None, tm, hv), lambda bi, i: (bi, i, 0)), _const((hv, d))],
        out_specs=pl.BlockSpec((None, tm, d), lambda bi, i: (bi, i, 0)),
        out_shape=jax.ShapeDtypeStruct((bsz, t, d), F32),
        compiler_params=_cparams(("parallel", "parallel")),
        name="attn_out_proj",
    )(h, o, w)


def _proj_kernel(h_ref, cs_ref, sn_ref, gkv_ref, gmix_ref, wkv_ref, glat_ref, gkr_ref, wdq_ref, gql_ref,
                 wqm_ref, wqr_ref, gq_ref, wuk_ref, gk_ref, wuv_ref,
                 c_ref, kr_ref, q_ref, k_ref, v_ref, *, nheads, nope, rope, scale):
    h = h_ref[...]
    hs = h * _rms_scale(h)
    hn = (hs * gkv_ref[...]).astype(BF16)
    xq = (hs * gmix_ref[...]).astype(BF16)
    z = _dot(hn, wkv_ref[...])
    cpre, a, ar = z[:, :LANES], z[:, LANES:2 * LANES], z[:, 2 * LANES:3 * LANES]
    c = cpre * _rms_scale(cpre) * glat_ref[...]
    c_ref[...] = c
    cs = cs_ref[...]
    sn = sn_ref[...]
    ra = lax.rsqrt(jnp.sum(a * a, axis=-1, keepdims=True) * (1.0 / rope) + NORM_EPS)
    kr = ra * (a * gkr_ref[...] * cs + ar * sn)
    kr_ref[...] = kr
    cb = c.astype(BF16)
    v_ref[...] = _dot(cb, wuv_ref[...]).astype(BF16)
    kn = _dot(cb, wuk_ref[...])
    ql = _dot(xq, wdq_ref[...])
    ql = (ql * _rms_scale(ql) * gql_ref[...]).astype(BF16)
    qm = _dot(ql, wqm_ref[...])
    qr = _dot(ql, wqr_ref[...])
    lane = lax.broadcasted_iota(jnp.int32, (1, HEAD_SLOT), 1)
    mn = (lane < nope).astype(F32)
    mr = jnp.logical_and(lane >= nope, lane < nope + rope).astype(F32)
    gq = gq_ref[...]
    gk = gk_ref[...]
    for hd in range(nheads):
        sl = slice(hd * HEAD_SLOT, (hd + 1) * HEAD_SLOT)
        qh = qm[:, sl]
        sq = qh * qh
        rn = lax.rsqrt(jnp.sum(sq * mn, axis=-1, keepdims=True) * (1.0 / nope) + NORM_EPS)
        rr = lax.rsqrt(jnp.sum(sq * mr, axis=-1, keepdims=True) * (1.0 / rope) + NORM_EPS)
        qv = (qh * gq * cs + qr[:, sl] * sn) * (mn * rn + mr * rr) * scale
        q_ref[:, sl] = qv.astype(BF16)
        kh = kn[:, sl]
        rk = lax.rsqrt(jnp.sum(kh * kh, axis=-1, keepdims=True) * (1.0 / nope) + NORM_EPS)
        k_ref[:, sl] = (kh * rk * gk + kr).astype(BF16)


def _proj_call(h, cs, sn, w, tm, nheads, nope, rope, vdim):
    bsz, t, d = h.shape
    ql = w["wdq"].shape[1]
    hs = nheads * HEAD_SLOT
    row = lambda n: pl.BlockSpec((None, tm, n), lambda bi, i: (bi, i, 0))
    tab = pl.BlockSpec((tm, HEAD_SLOT), lambda bi, i: (i, 0))
    in_specs = [row(d), tab, tab, _const((1, d)), _const((1, d)), _const((d, 3 * LANES)), _const((1, LANES)),
                _const((1, HEAD_SLOT)), _const((d, ql)), _const((1, ql)), _const((ql, hs)), _const((ql, hs)),
                _const((1, HEAD_SLOT)), _const((LANES, hs)), _const((1, HEAD_SLOT)), _const((LANES, nheads * vdim))]
    return pl.pallas_call(
        functools.partial(_proj_kernel, nheads=nheads, nope=nope, rope=rope,
                          scale=LOG2E / math.sqrt(nope + rope)),
        grid=(bsz, t // tm),
        in_specs=in_specs,
        out_specs=[row(LANES), row(HEAD_SLOT), row(hs), row(hs), row(nheads * vdim)],
        out_shape=[jax.ShapeDtypeStruct((bsz, t, LANES), F32), jax.ShapeDtypeStruct((bsz, t, HEAD_SLOT), F32),
                   jax.ShapeDtypeStruct((bsz, t, hs), BF16), jax.ShapeDtypeStruct((bsz, t, hs), BF16),
                   jax.ShapeDtypeStruct((bsz, t, nheads * vdim), BF16)],
        compiler_params=_cparams(("parallel", "parallel")),
        name="mla_proj",
    )(h, cs, sn, w["gkv"], w["gmix"], w["wkv"], w["glat"], w["gkr"], w["wdq"], w["gql"], w["wqm"], w["wqr"],
      w["gq"], w["wuk"], w["gk"], w["wuv"])


def _attn_update(hd, r0, nr, q_ref, kb, vb, mask, m_ref, l_ref, acc_ref, first):
    rows = slice(r0, r0 + nr)
    q = q_ref[rows, hd * HEAD_SLOT:(hd + 1) * HEAD_SLOT]
    s = _dot_nt(q, kb)
    if mask is not None:
        s = jnp.where(mask, s, NEG)
    nk = s.shape[1]
    m_cur = jnp.max(s, axis=1, keepdims=True)
    if first:
        m_new = jnp.broadcast_to(m_cur, (nr, LANES))
    else:
        m_prev = m_ref[hd, rows, :]
        m_new = jnp.maximum(m_prev, m_cur)
    if nk % LANES == 0:
        mb = jnp.concatenate([m_new] * (nk // LANES), axis=1) if nk > LANES else m_new
    else:
        mb = m_new[:, :nk]
    p = jnp.exp2(s - mb)
    ps = jnp.sum(p, axis=1, keepdims=True)
    pv = _dot(p.astype(BF16), vb)
    if first:
        l_ref[hd, rows, :] = jnp.broadcast_to(ps, (nr, LANES))
        acc_ref[hd, rows, :] = pv
    else:
        alpha = jnp.exp2(m_prev - m_new)
        l_ref[hd, rows, :] = alpha * l_ref[hd, rows, :] + ps
        acc_ref[hd, rows, :] = alpha * acc_ref[hd, rows, :] + pv
    m_ref[hd, rows, :] = m_new


def _attn_kernel(*refs, tq, tk, dk, has_prefix, vdim, hps):
    if has_prefix:
        q_ref, k_ref, v_ref, pk_ref, pv_ref, o_ref, m_ref, l_ref, acc_ref = refs
    else:
        q_ref, k_ref, v_ref, o_ref, m_ref, l_ref, acc_ref = refs
    i = pl.program_id(2)
    kcols = lambda hd: slice(hd * HEAD_SLOT, (hd + 1) * HEAD_SLOT)
    vcols = lambda hd: slice((hd // 2) * LANES, (hd // 2 + 1) * LANES)
    if has_prefix:
        for hd in range(hps):
            _attn_update(hd, 0, tq, q_ref, pk_ref[:, kcols(hd)], pv_ref[:, vcols(hd)], None,
                         m_ref, l_ref, acc_ref, True)
        nfull = i * (tq // tk)

        def body(j, carry):
            k0 = pl.multiple_of(j * tk, tk)
            for hd in range(hps):
                _attn_update(hd, 0, tq, q_ref, k_ref[pl.ds(k0, tk), kcols(hd)], v_ref[pl.ds(k0, tk), vcols(hd)],
                             None, m_ref, l_ref, acc_ref, False)
            return carry

        lax.fori_loop(0, nfull, body, 0)
    for jj in range(tq // dk):
        r0 = jj * dk
        nr = tq - r0
        k0 = pl.multiple_of(i * tq + r0, dk)
        row = lax.broadcasted_iota(jnp.int32, (nr, dk), 0)
        col = lax.broadcasted_iota(jnp.int32, (nr, dk), 1)
        mask = col <= row
        for hd in range(hps):
            _attn_update(hd, r0, nr, q_ref, k_ref[pl.ds(k0, dk), kcols(hd)], v_ref[pl.ds(k0, dk), vcols(hd)], mask,
                         m_ref, l_ref, acc_ref, (not has_prefix) and jj == 0)
    lane = lax.broadcasted_iota(jnp.int32, (tq, LANES), 1)
    for pr in range(hps // 2):
        o0 = acc_ref[2 * pr] / l_ref[2 * pr]
        o1 = acc_ref[2 * pr + 1] / l_ref[2 * pr + 1]
        o_ref[:, pr * LANES:(pr + 1) * LANES] = jnp.where(lane < vdim, o0, o1).astype(o_ref.dtype)


def _attn_call(q, k, v, pk, pv, tq, tk, dk, nheads, vdim):
    bsz, t, _ = q.shape
    hps = ATTN_HEADS_PER_STEP if nheads % ATTN_HEADS_PER_STEP == 0 else 2
    assert 2 * vdim == LANES and nheads % hps == 0 and t % tq == 0 and tq % tk == 0 and tq % dk == 0
    has_prefix = pk is not None
    assert has_prefix or t == tq
    in_specs = [pl.BlockSpec((None, tq, hps * HEAD_SLOT), lambda bi, hg, i: (bi, i, hg)),
                pl.BlockSpec((None, t, hps * HEAD_SLOT), lambda bi, hg, i: (bi, 0, hg)),
                pl.BlockSpec((None, t, hps * vdim), lambda bi, hg, i: (bi, 0, hg))]
    args = [q, k, v]
    if has_prefix:
        npre = pk.shape[0]
        in_specs += [pl.BlockSpec((npre, hps * HEAD_SLOT), lambda bi, hg, i: (0, hg)),
                     pl.BlockSpec((npre, hps * vdim), lambda bi, hg, i: (0, hg))]
        args += [pk, pv]
    return pl.pallas_call(
        functools.partial(_attn_kernel, tq=tq, tk=tk, dk=dk, has_prefix=has_prefix, vdim=vdim, hps=hps),
        grid=(bsz, nheads // hps, t // tq),
        in_specs=in_specs,
        out_specs=pl.BlockSpec((None, tq, hps * vdim), lambda bi, hg, i: (bi, i, hg)),
        out_shape=jax.ShapeDtypeStruct((bsz, t, nheads * vdim), BF16),
        scratch_shapes=[pltpu.VMEM((hps, tq, LANES), F32)] * 3,
        compiler_params=_cparams(("parallel", "parallel", "arbitrary")),
        name="mla_prompt_attn",
    )(*args)


def _qt_kernel(q_ref, w_ref, o_ref, *, nheads):
    for hd in range(nheads):
        o_ref[hd] = _dot(q_ref[:, hd * HEAD_SLOT:(hd + 1) * HEAD_SLOT], w_ref[hd]).astype(BF16)


def _qt_call(q, w, nheads):
    r = q.shape[0]
    return pl.pallas_call(
        functools.partial(_qt_kernel, nheads=nheads),
        out_shape=jax.ShapeDtypeStruct((nheads, r, LANES), BF16),
        compiler_params=pltpu.CompilerParams(vmem_limit_bytes=VMEM_LIMIT),
        name="mla_absorb_q",
    )(q, w)


def _uv_kernel(ol_ref, w_ref, o_ref, *, nheads):
    for hp in range(nheads // 2):
        r = (_dot(ol_ref[2 * hp].astype(BF16), w_ref[2 * hp])
             + _dot(ol_ref[2 * hp + 1].astype(BF16), w_ref[2 * hp + 1]))
        o_ref[:, hp * LANES:(hp + 1) * LANES] = r.astype(BF16)


def _uv_call(ol, w, nheads, vdim):
    r = ol.shape[1]
    return pl.pallas_call(
        functools.partial(_uv_kernel, nheads=nheads),
        out_shape=jax.ShapeDtypeStruct((r, nheads * vdim), BF16),
        compiler_params=pltpu.CompilerParams(vmem_limit_bytes=VMEM_LIMIT),
        name="mla_value_up",
    )(ol, w)


def _paged_scores(wukt_ref, qt_ref, qr_ref, cb, krt, nheads, nq, nope):
    n = cb.shape[0]
    kt = _dot_nt(wukt_ref[...], cb)
    ss = jnp.sum((kt * kt).reshape(nheads, nope, n), axis=1)
    rs = lax.rsqrt(ss * (1.0 / nope) + NORM_EPS)
    sraw = _dot_nt(qt_ref[...], cb)
    srope = _dot(qr_ref[...], krt)
    return (sraw.reshape(nq, nheads, n) * rs[None, :, :]).reshape(nq * nheads, n) + srope


def _paged_kernel(pt_ref, lat_hbm, krt_hbm, wukt_ref, qt_ref, qr_ref, cn_ref, krn_ref, o_ref,
                  latbuf, krbuf, sem, m_ref, l_ref, acc_ref, *, npages, ch, sub, nheads, nq, nope):
    b = pl.program_id(0)
    nb = pl.num_programs(0)
    nc = npages // ch
    page = latbuf.shape[1] // ch
    nrow = nq * nheads

    def lat_copy(pid, slot, p):
        return pltpu.make_async_copy(lat_hbm.at[pid], latbuf.at[slot, pl.ds(p * page, page), :], sem.at[slot, 0])

    def krt_copy(pid, slot, p):
        return pltpu.make_async_copy(krt_hbm.at[pid], krbuf.at[slot, :, pl.ds(p * page, page)], sem.at[slot, 1])

    def start_chunk(bb, c, slot):
        for p in range(ch):
            pid = pt_ref[bb * npages + c * ch + p]
            lat_copy(pid, slot, p).start()
            krt_copy(pid, slot, p).start()

    def wait_chunk(slot):
        for p in range(ch):
            lat_copy(0, slot, p).wait()
            krt_copy(0, slot, p).wait()

    def update(s, cb):
        n = s.shape[1]
        m_prev = m_ref[...]
        m_new = jnp.maximum(m_prev, jnp.max(s, axis=1, keepdims=True))
        p = jnp.exp2(s - jnp.concatenate([m_new] * (n // LANES), axis=1))
        alpha = jnp.exp2(m_prev - m_new)
        l_ref[...] = alpha * l_ref[...] + jnp.sum(p, axis=1, keepdims=True)
        acc_ref[...] = alpha * acc_ref[...] + _dot(p.astype(BF16), cb)
        m_ref[...] = m_new

    def compute(slot):
        cbs, scores = [], []
        for sb in range(ch * page // sub):
            keys = slice(sb * sub, (sb + 1) * sub)
            cbs.append(latbuf[slot, keys, :].astype(BF16))
            krt = krbuf[slot, :, keys].astype(BF16)
            scores.append(_paged_scores(wukt_ref, qt_ref, qr_ref, cbs[-1], krt, nheads, nq, nope))
        update(jnp.concatenate(scores, axis=1), jnp.concatenate(cbs, axis=0))

    @pl.when(b == 0)
    def _():
        start_chunk(0, 0, 0)

    cb = cn_ref[...].astype(BF16)
    s = _paged_scores(wukt_ref, qt_ref, qr_ref, cb, krn_ref[...].astype(BF16), nheads, nq, nope)
    n = s.shape[1]
    row = lax.broadcasted_iota(jnp.int32, (nrow, n), 0)
    col = lax.broadcasted_iota(jnp.int32, (nrow, n), 1)
    s = jnp.where(col * nheads <= row, s, NEG)
    m = jnp.max(s, axis=1, keepdims=True)
    p = jnp.exp2(s - m)
    m_ref[...] = jnp.broadcast_to(m, (nrow, LANES))
    l_ref[...] = jnp.broadcast_to(jnp.sum(p, axis=1, keepdims=True), (nrow, LANES))
    acc_ref[...] = _dot(p.astype(BF16), cb)

    def pair_body(cp, carry):
        c0 = 2 * cp
        start_chunk(b, c0 + 1, 1)
        wait_chunk(0)
        compute(0)

        @pl.when(c0 + 2 < nc)
        def _():
            start_chunk(b, c0 + 2, 0)

        @pl.when(jnp.logical_and(c0 + 2 == nc, b + 1 < nb))
        def _():
            start_chunk(b + 1, 0, 0)

        wait_chunk(1)
        compute(1)
        return carry

    lax.fori_loop(0, nc // 2, pair_body, 0)
    o_ref[...] = acc_ref[...] / l_ref[...]


def _paged_call(page_table, cache_lat, cache_krt, wukt, qt, qr, cn, krn, nheads, nq, nope):
    db, npages = page_table.shape
    _, page, clat = cache_lat.shape
    rdim = cache_krt.shape[1]
    ch = PAGES_PER_CHUNK
    while npages % (2 * ch):
        ch //= 2
    sub = min(PAGED_SUB_KEYS, ch * page)
    assert ch >= 1 and (ch * page) % sub == 0
    nrow = nq * nheads
    per_b = lambda shape: pl.BlockSpec((None,) + shape, lambda b, pt: (b, 0, 0))
    in_specs = [pl.BlockSpec(memory_space=pl.ANY), pl.BlockSpec(memory_space=pl.ANY),
                pl.BlockSpec(wukt.shape, lambda b, pt: (0, 0)),
                per_b((nrow, clat)), per_b((nrow, rdim)), per_b(cn.shape[1:]), per_b(krn.shape[1:])]
    return pl.pallas_call(
        functools.partial(_paged_kernel, npages=npages, ch=ch, sub=sub, nheads=nheads, nq=nq, nope=nope),
        grid_spec=pltpu.PrefetchScalarGridSpec(
            num_scalar_prefetch=1,
            grid=(db,),
            in_specs=in_specs,
            out_specs=per_b((nrow, clat)),
            scratch_shapes=[pltpu.VMEM((2, ch * page, clat), F32), pltpu.VMEM((2, rdim, ch * page), F32),
                            pltpu.SemaphoreType.DMA((2, 2))] + [pltpu.VMEM((nrow, LANES), F32)] * 3),
        out_shape=jax.ShapeDtypeStruct((db, nrow, clat), F32),
        compiler_params=_cparams(("arbitrary",)),
        name="mla_paged_attn",
    )(page_table.reshape(-1), cache_lat, cache_krt, wukt, qt, qr, cn, krn)


def _slot(x, lo, width=HEAD_SLOT):
    n = x.shape[-1]
    pad = [(0, 0)] * (x.ndim - 1) + [(lo, width - lo - n)]
    return jnp.pad(x, pad)


def _half_rot_cols(w):
    half = w.shape[-1] // 2
    return jnp.concatenate([-w[..., half:], w[..., :half]], axis=-1)


def _rope_tables(pos, rope, nope):
    half = rope // 2
    inv = ROPE_THETA ** (-jnp.arange(half, dtype=F32) / half)
    ang = pos.astype(F32)[:, None] * inv[None, :]
    cos, sin = jnp.cos(ang), jnp.sin(ang)
    cs = jnp.concatenate([jnp.ones((pos.shape[0], nope), F32), cos, cos], axis=1)
    sn = jnp.concatenate([jnp.zeros((pos.shape[0], nope), F32), sin, sin], axis=1)
    return _slot(cs, 0), _slot(sn, 0)


def kernel(x_prompt, x_sample, state_conv_a, state_ffn_conv, cache_kv_latent, cache_k_rope, page_table, meta_tokens, norm_mix, norm_ffn, a_w_pw1, a_b_pw1, a_w_dw, a_b_dw, a_ln_g, a_ln_b, a_w_pw2, a_b_pw2, ffn_w_up, ffn_w_dw, ffn_w_down, kv_norm, mla_w_dkv, mla_lat_norm, mla_w_kr, mla_knorm_rope, mla_w_uk, mla_w_uv, mla_knorm_nope, mla_w_dq, mla_q_lat_norm, mla_w_uq, mla_qnorm_nope, mla_qnorm_rope, mla_w_o):
    bsz, seq, d = x_prompt.shape
    db, nq, _ = x_sample.shape
    nmeta = meta_tokens.shape[0]
    depth = norm_mix.shape[0]
    n_a = a_w_pw1.shape[0]
    n_b = mla_w_dq.shape[0]
    f = ffn_w_down.shape[1]
    clat, nheads, nope = mla_w_uk.shape
    rope = mla_w_kr.shape[1]
    vdim = mla_w_uv.shape[2]
    width_a = a_w_dw.shape[1]
    npages, page = page_table.shape[1], cache_kv_latent.shape[1]
    past_len = npages * page
    assert n_b == 1 and depth == n_a + n_b and clat == LANES and nope + rope <= HEAD_SLOT
    assert seq >= width_a - 1 and nmeta % 16 == 0 and nmeta >= 8
    tm = ROW_TILE if seq % ROW_TILE == 0 else seq
    rs = nq * db
    row2 = lambda v: v.reshape(1, -1)

    hp = x_prompt
    hm = meta_tokens.astype(F32)[None]
    hs = jnp.transpose(x_sample, (1, 0, 2)).reshape(1, rs, d)

    conv_a_p, conv_a_s, ffn_p, ffn_s = [], [], [], []

    def run_ffn(layer, hp, hm, hs):
        g = row2(norm_ffn[layer])
        wup = jnp.transpose(ffn_w_up[layer].reshape(d, 2, f), (1, 0, 2)).astype(BF16)
        wdw = jnp.transpose(ffn_w_dw[layer].reshape(3, 2, f), (1, 0, 2))
        wdn = ffn_w_down[layer].astype(BF16)
        st = jnp.transpose(state_ffn_conv[layer].reshape(db, 2, 2, f), (2, 1, 0, 3))
        hm_new, _ = _ffn_call(hm, g, wup, wdw, wdn, nmeta)
        hp_new, tail_p = _ffn_call(hp, g, wup, wdw, wdn, tm, prev=hm[0, nmeta - 8:])
        hs_new, tail_s = _ffn_call(hs, g, wup, wdw, wdn, rs, state=st)
        ffn_p.append(jnp.transpose(tail_p[:, :, 6:8, :], (0, 2, 1, 3)).reshape(bsz, 2, 2 * f))
        ffn_s.append(jnp.transpose(tail_s[0].reshape(2, 2, db, f), (2, 1, 0, 3)).reshape(db, 2, 2 * f))
        return hp_new, hm_new, hs_new

    for layer in range(n_a):
        g = row2(norm_mix[layer])
        w1 = a_w_pw1[layer].astype(BF16)
        b1 = row2(a_b_pw1[layer])
        conv_w = (a_w_dw[layer], row2(a_b_dw[layer]), row2(a_ln_g[layer]), row2(a_ln_b[layer]),
                  a_w_pw2[layer].astype(BF16), row2(a_b_pw2[layer]))
        um = _a1_call(hm, g, w1, b1, nmeta)
        up = _a1_call(hp, g, w1, b1, tm)
        us = _a1_call(hs, g, w1, b1, rs)
        halo = jnp.pad(um[0], ((32 - nmeta, 0), (0, 0))) if nmeta < 32 else um[0, nmeta - 32:]
        hm_new = _a2_seq_call(um, hm, None, *conv_w, nmeta)
        hp = _a2_seq_call(up, hp, halo, *conv_w, tm)
        hm = hm_new
        xp_s = jnp.concatenate([jnp.transpose(state_conv_a[layer], (1, 0, 2)), us.reshape(nq, db, d)], axis=0)
        hs = _a2_tm_call(xp_s, hs.reshape(nq, db, d), *conv_w, 32 if db % 32 == 0 else db).reshape(1, rs, d)
        conv_a_p.append(up[:, seq - (width_a - 1):])
        conv_a_s.append(jnp.transpose(xp_s[nq:], (1, 0, 2)))
        hp, hm, hs = run_ffn(layer, hp, hm, hs)

    layer = n_a
    g_kr = mla_knorm_rope
    wkr_g = mla_w_kr * g_kr[None, :]
    uq = mla_w_uq[0]
    uq_rope_g = uq[:, :, nope:] * mla_qnorm_rope[0][None, None, :]
    pw = {
        "gkv": row2(kv_norm), "gmix": row2(norm_mix[layer]),
        "wkv": jnp.concatenate([mla_w_dkv, _slot(mla_w_kr, nope), _slot(_half_rot_cols(wkr_g), nope)], axis=1).astype(BF16),
        "glat": row2(mla_lat_norm), "gkr": _slot(row2(g_kr), nope),
        "wdq": mla_w_dq[0].astype(BF16), "gql": row2(mla_q_lat_norm[0]),
        "wqm": _slot(uq, 0).reshape(uq.shape[0], nheads * HEAD_SLOT).astype(BF16),
        "wqr": _slot(_half_rot_cols(uq_rope_g), nope).reshape(uq.shape[0], nheads * HEAD_SLOT).astype(BF16),
        "gq": _slot(row2(jnp.concatenate([mla_qnorm_nope[0], mla_qnorm_rope[0]])), 0),
        "wuk": _slot(mla_w_uk, 0).reshape(clat, nheads * HEAD_SLOT).astype(BF16),
        "gk": _slot(row2(mla_knorm_nope), 0),
        "wuv": mla_w_uv.reshape(clat, nheads * vdim).astype(BF16),
    }
    cs_m, sn_m = _rope_tables(jnp.arange(nmeta), rope, nope)
    cs_p, sn_p = _rope_tables(nmeta + jnp.arange(seq), rope, nope)
    cs_s, sn_s = _rope_tables(jnp.repeat(past_len + jnp.arange(nq), db), rope, nope)
    proj = functools.partial(_proj_call, w=pw, nheads=nheads, nope=nope, rope=rope, vdim=vdim)
    c_m, kr_m, q_m, k_m, v_m = proj(hm, cs_m, sn_m, tm=nmeta)
    c_p, kr_p, q_p, k_p, v_p = proj(hp, cs_p, sn_p, tm=tm)
    c_s, kr_s, q_s, _, _ = proj(hs, cs_s, sn_s, tm=rs)
    kr_m, kr_p, kr_s = (x[..., nope:nope + rope] for x in (kr_m, kr_p, kr_s))

    tq = ATTN_Q_TILE if seq % ATTN_Q_TILE == 0 else seq
    tk = ATTN_K_TILE if tq % ATTN_K_TILE == 0 else tq
    dk = ATTN_DIAG_TILE if tq % ATTN_DIAG_TILE == 0 else tq
    o_m = _attn_call(q_m, k_m, v_m, None, None, nmeta, nmeta, nmeta, nheads, vdim)
    o_p = _attn_call(q_p, k_p, v_p, k_m[0], v_m[0], tq, tk, dk, nheads, vdim)

    wukt = jnp.transpose(mla_w_uk.reshape(clat, nheads * nope)).astype(BF16)
    wqt = jnp.transpose(mla_w_uk, (1, 2, 0)) * mla_knorm_nope[None, :, None]
    wqt = jnp.pad(wqt, ((0, 0), (0, HEAD_SLOT - nope), (0, 0))).astype(BF16)
    qt = _qt_call(q_s[0], wqt, nheads)
    qt = jnp.transpose(qt.reshape(nheads, nq, db, clat), (2, 1, 0, 3)).reshape(db, nq * nheads, clat)
    qr = jnp.transpose(q_s[0].reshape(nq, db, nheads, HEAD_SLOT)[..., nope:nope + rope], (1, 0, 2, 3))
    qr = qr.reshape(db, nq * nheads, rope)
    nnew = LANES
    cn = jnp.pad(jnp.transpose(c_s[0].reshape(nq, db, clat), (1, 0, 2)), ((0, 0), (0, nnew - nq), (0, 0)))
    krn = jnp.pad(jnp.transpose(kr_s[0].reshape(nq, db, rope), (1, 2, 0)), ((0, 0), (0, 0), (0, nnew - nq)))
    cache_krt = jnp.transpose(cache_k_rope, (0, 2, 1))
    o_lat = _paged_call(page_table, cache_kv_latent, cache_krt, wukt, qt, qr, cn, krn, nheads, nq, nope)
    ol = jnp.transpose(o_lat.reshape(db, nq, nheads, clat), (2, 1, 0, 3)).reshape(nheads, rs, clat)
    wuv_pad = jnp.stack([_slot(mla_w_uv[:, hd, :], (hd % 2) * vdim, LANES) for hd in range(nheads)]).astype(BF16)
    o_s = _uv_call(ol, wuv_pad, nheads, vdim)[None]

    wo = mla_w_o[0].astype(BF16)
    hm = _wo_call(hm, o_m, wo, nmeta)
    hp = _wo_call(hp, o_p, wo, tm)
    hs = _wo_call(hs, o_s, wo, rs)
    hp, hm, hs = run_ffn(layer, hp, hm, hs)

    y_sample = jnp.transpose(hs.reshape(nq, db, d), (1, 0, 2))
    kv_lat_p = jnp.concatenate([jnp.broadcast_to(c_m, (bsz, nmeta, clat)), c_p], axis=1)
    k_rope_p = jnp.concatenate([jnp.broadcast_to(kr_m, (bsz, nmeta, rope)), kr_p], axis=1)
    kv_lat_s = jnp.transpose(c_s[0].reshape(nq, db, clat), (1, 0, 2))
    k_rope_s = jnp.transpose(kr_s[0].reshape(nq, db, rope), (1, 0, 2))
    return (hp, y_sample, jnp.stack(conv_a_p), jnp.stack(conv_a_s), jnp.stack(ffn_p), jnp.stack(ffn_s),
            kv_lat_p, k_rope_p, kv_lat_s, k_rope_s)
```

```python
import functools
import math

import jax
import jax.numpy as jnp
from jax import lax
from jax.experimental import pallas as pl
from jax.experimental.pallas import tpu as pltpu

F32 = jnp.float32
BF16 = jnp.bfloat16
NORM_EPS = 1e-6
ROPE_THETA = 10000.0
NEG = -1e30

LANES = 128
HEAD_SLOT = 128
ROW_TILE = 512
ATTN_Q_TILE = 1024
ATTN_K_TILE = 512
ATTN_DIAG_TILE = 512
FFN_CHUNK = 256
FFN_SEQ_CHUNK = 2816
ATTN_HEADS_PER_STEP = 4
PAGES_PER_CHUNK = 64
PAGED_SUB_KEYS = 1024
LOG2E = 1.4426950408889634
VMEM_LIMIT = 56 * 1024 * 1024


def _cparams(sem):
    return pltpu.CompilerParams(dimension_semantics=sem, vmem_limit_bytes=VMEM_LIMIT)


def _const(shape):
    zeros = (0,) * len(shape)
    return pl.BlockSpec(shape, lambda *_: zeros)


def _sigmoid(x):
    return 1.0 / (1.0 + jnp.exp(-x))


def _rms_scale(x):
    return lax.rsqrt(jnp.mean(x * x, axis=-1, keepdims=True) + NORM_EPS)


def _dot(a, b):
    return jnp.dot(a, b, preferred_element_type=F32)


def _dot_nt(a, b):
    return lax.dot_general(a, b, (((1,), (1,)), ((), ())), preferred_element_type=F32)


def _a1_kernel(x_ref, g_ref, w_ref, b_ref, u_ref):
    x = x_ref[...]
    xn = x * _rms_scale(x) * g_ref[...]
    z = _dot(xn.astype(BF16), w_ref[...]) + b_ref[...]
    d = u_ref.shape[-1]
    u_ref[...] = z[:, :d] * _sigmoid(z[:, d:])


def _a1_call(x, g, w, b, tm):
    bsz, t, d = x.shape
    return pl.pallas_call(
        _a1_kernel,
        grid=(bsz, t // tm),
        in_specs=[pl.BlockSpec((None, tm, d), lambda bi, i: (bi, i, 0)),
                  _const((1, d)), _const((d, 2 * d)), _const((1, 2 * d))],
        out_specs=pl.BlockSpec((None, tm, d), lambda bi, i: (bi, i, 0)),
        out_shape=jax.ShapeDtypeStruct((bsz, t, d), F32),
        compiler_params=_cparams(("parallel", "parallel")),
        name="conv_pw1_glu",
    )(x, g, w, b)


def _ln_swish_pw2(y, h, lng_ref, lnb_ref, w2_ref, b2_ref):
    mu = jnp.mean(y, axis=-1, keepdims=True)
    yc = y - mu
    var = jnp.mean(yc * yc, axis=-1, keepdims=True)
    yn = yc * lax.rsqrt(var + NORM_EPS) * lng_ref[...] + lnb_ref[...]
    a = yn * _sigmoid(yn)
    return h + _dot(a.astype(BF16), w2_ref[...]) + b2_ref[...]


def _a2_seq_kernel(*refs, tm, halo, width, has_prev):
    if has_prev:
        (u_ref, uh_ref, prev_ref, h_ref, wdw_ref, bdw_ref, lng_ref, lnb_ref, w2_ref, b2_ref,
         o_ref, ext_ref, y_ref) = refs
        i = pl.program_id(1)

        @pl.when(i == 0)
        def _():
            ext_ref[0:halo, :] = prev_ref[...]

        @pl.when(i > 0)
        def _():
            ext_ref[0:halo, :] = uh_ref[...]
    else:
        (u_ref, h_ref, wdw_ref, bdw_ref, lng_ref, lnb_ref, w2_ref, b2_ref,
         o_ref, ext_ref, y_ref) = refs
        ext_ref[0:halo, :] = jnp.zeros((halo, ext_ref.shape[1]), F32)
    ext_ref[halo:halo + tm, :] = u_ref[...]
    d = u_ref.shape[-1]
    off = halo - (width - 1)

    taps = {}
    for k in range(width):
        taps.setdefault((off + k) % 8, []).append(((off + k) // 8, k))
    nblk = tm // 8

    def col_body(c, carry):
        c0 = pl.multiple_of(c * LANES, LANES)
        cols = pl.ds(c0, LANES)
        bias = jnp.broadcast_to(bdw_ref[:, cols], (8, LANES))
        sub = lax.broadcasted_iota(jnp.int32, (8, LANES), 0)
        xs = {}

        def xblk(jb):
            if jb not in xs:
                xs[jb] = ext_ref[pl.ds(jb * 8, 8), cols]
            return xs[jb]

        def zsum(r, jb):
            acc = None
            for a, k in taps[r]:
                term = wdw_ref[pl.ds(k * 8, 8), cols] * xblk(jb + a)
                acc = term if acc is None else acc + term
            return acc

        zprev = {r: zsum(r, 0) for r in taps if r != 0}
        for jb in range(nblk):
            y = zsum(0, jb) + bias if 0 in taps else bias
            znext = {}
            for r in zprev:
                znext[r] = zsum(r, jb + 1)
                y = y + pltpu.roll(jnp.where(sub >= r, zprev[r], znext[r]), 8 - r, axis=0)
            y_ref[pl.ds(jb * 8, 8), cols] = y
            zprev = znext
        return carry

    lax.fori_loop(0, d // LANES, col_body, 0)
    o_ref[...] = _ln_swish_pw2(y_ref[...], h_ref[...], lng_ref, lnb_ref, w2_ref, b2_ref)


def _a2_seq_call(u, h, prev, wdw, bdw, lng, lnb, w2, b2, tm):
    bsz, t, d = u.shape
    width = wdw.shape[0]
    halo = 32
    assert halo >= width - 1 and halo % 8 == 0 and (tm % halo == 0 or prev is None)
    wdw = jnp.repeat(wdw, 8, axis=0)
    row = pl.BlockSpec((None, tm, d), lambda bi, i: (bi, i, 0))
    consts = [_const((width * 8, d)), _const((1, d)), _const((1, d)), _const((1, d)), _const((d, d)), _const((1, d))]
    if prev is not None:
        k = tm // halo
        in_specs = [row, pl.BlockSpec((None, halo, d), lambda bi, i: (bi, jnp.maximum(i * k - 1, 0), 0)),
                    _const((halo, d)), row] + consts
        args = (u, u, prev, h, wdw, bdw, lng, lnb, w2, b2)
    else:
        assert t == tm
        in_specs = [row, row] + consts
        args = (u, h, wdw, bdw, lng, lnb, w2, b2)
    return pl.pallas_call(
        functools.partial(_a2_seq_kernel, tm=tm, halo=halo, width=width, has_prev=prev is not None),
        grid=(bsz, t // tm),
        in_specs=in_specs,
        out_specs=row,
        out_shape=jax.ShapeDtypeStruct((bsz, t, d), F32),
        scratch_shapes=[pltpu.VMEM((halo + tm, d), F32), pltpu.VMEM((tm, d), F32)],
        compiler_params=_cparams(("parallel", "parallel")),
        name="conv_dw_ln_pw2",
    )(*args)


def _a2_tm_kernel(xp_ref, h_ref, wdw_ref, bdw_ref, lng_ref, lnb_ref, w2_ref, b2_ref, o_ref, y_ref, *, width, cw):
    nt, bs, d = h_ref.shape
    for c in range(d // cw):
        cols = slice(c * cw, (c + 1) * cw)
        wk = [jnp.broadcast_to(wdw_ref[k:k + 1, cols], (bs, cw)) for k in range(width)]
        bias = bdw_ref[:, cols]
        for t in range(nt):
            acc = wk[0] * xp_ref[t, :, cols]
            for k in range(1, width):
                acc = acc + wk[k] * xp_ref[t + k, :, cols]
            y_ref[t * bs:(t + 1) * bs, cols] = acc + bias
    h = h_ref[...].reshape(nt * bs, d)
    out = _ln_swish_pw2(y_ref[...], h, lng_ref, lnb_ref, w2_ref, b2_ref)
    o_ref[...] = out.reshape(nt, bs, d)


def _a2_tm_call(xp, h, wdw, bdw, lng, lnb, w2, b2, bs):
    nt, db, d = h.shape
    width = wdw.shape[0]
    return pl.pallas_call(
        functools.partial(_a2_tm_kernel, width=width, cw=2 * LANES),
        grid=(db // bs,),
        in_specs=[pl.BlockSpec((nt + width - 1, bs, d), lambda i: (0, i, 0)),
                  pl.BlockSpec((nt, bs, d), lambda i: (0, i, 0)),
                  _const((width, d)), _const((1, d)), _const((1, d)), _const((1, d)), _const((d, d)), _const((1, d))],
        out_specs=pl.BlockSpec((nt, bs, d), lambda i: (0, i, 0)),
        out_shape=jax.ShapeDtypeStruct((nt, db, d), F32),
        scratch_shapes=[pltpu.VMEM((nt * bs, d), F32)],
        compiler_params=_cparams(("parallel",)),
        name="conv_dw_ln_pw2_sample",
    )(xp, h, wdw, bdw, lng, lnb, w2, b2)


def _ffn_kernel(*refs, mode, tm, fc, shift, ngroups):
    if mode == "prev":
        x_ref, xh_ref, prev_ref, g_ref, wup_ref, wdw_ref, wdn_ref, o_ref, tail_ref = refs
    elif mode == "zero":
        x_ref, g_ref, wup_ref, wdw_ref, wdn_ref, o_ref, tail_ref = refs
    else:
        x_ref, st_ref, g_ref, wup_ref, wdw_ref, wdn_ref, o_ref, tail_ref = refs
    x = x_ref[...]
    d = x.shape[-1]
    f = wdn_ref.shape[0]
    if mode == "prev":
        first = pl.program_id(1) == 0
        halo = jnp.where(first, prev_ref[...], xh_ref[...])
        xe = jnp.concatenate([halo, x], axis=0)
    elif mode == "zero":
        xe = jnp.concatenate([jnp.zeros((8, d), F32), x], axis=0)
    else:
        xe = x
    pre = xe.shape[0] - tm if mode != "state" else 2 * shift
    xn = (xe * _rms_scale(xe) * g_ref[...]).astype(BF16)
    ntail = tail_ref.shape[1]
    fg = wdn_ref.shape[0]
    acc = jnp.zeros((tm, d), F32)
    for j in range(fg // fc):
        cols = slice(j * fc, (j + 1) * fc)
        act = None
        for gu in range(2):
            hu = _dot(xn, wup_ref[gu, :, cols])
            tail_ref[gu, :, cols] = hu[hu.shape[0] - ntail:, :]
            if mode == "state":
                hu = jnp.concatenate([st_ref[gu, 0, :, cols], st_ref[gu, 1, :, cols], hu], axis=0)
            w = wdw_ref[gu, :, cols]
            conv = (w[0:1] * hu[pre - 2 * shift:pre - 2 * shift + tm]
                    + w[1:2] * hu[pre - shift:pre - shift + tm]
                    + w[2:3] * hu[pre:pre + tm])
            act = conv * _sigmoid(conv) if gu == 0 else act * conv
        acc = acc + _dot(act.astype(BF16), wdn_ref[cols, :])
    if ngroups == 1:
        o_ref[...] = x + acc
    else:
        gj = pl.program_id(2)

        @pl.when(gj == 0)
        def _():
            o_ref[...] = x + acc

        @pl.when(gj > 0)
        def _():
            o_ref[...] = o_ref[...] + acc


def _ffn_call(x, g, wup, wdw, wdn, tm, prev=None, state=None):
    bsz, t, d = x.shape
    f = wdn.shape[0]
    fc = FFN_CHUNK if f % FFN_CHUNK == 0 else LANES
    if state is None and f % FFN_SEQ_CHUNK == 0:
        fc = FFN_SEQ_CHUNK
    fg = fc if state is not None else f
    ngroups = f // fg
    row = pl.BlockSpec((None, tm, d), lambda bi, i, gj: (bi, i, 0))
    consts = [pl.BlockSpec((1, d), lambda bi, i, gj: (0, 0)),
              pl.BlockSpec((2, d, fg), lambda bi, i, gj: (0, 0, gj)),
              pl.BlockSpec((2, 3, fg), lambda bi, i, gj: (0, 0, gj)),
              pl.BlockSpec((fg, d), lambda bi, i, gj: (gj, 0))]
    if state is not None:
        assert bsz == 1 and t == tm
        mode, shift, ntail = "state", state.shape[2], 2 * state.shape[2]
        in_specs = [row, pl.BlockSpec(state.shape[:3] + (fg,), lambda bi, i, gj: (0, 0, 0, gj))] + consts
        args = (x, state, g, wup, wdw, wdn)
    elif prev is not None:
        mode, shift, ntail = "prev", 1, 8
        k = tm // 8
        in_specs = [row, pl.BlockSpec((None, 8, d), lambda bi, i, gj: (bi, jnp.maximum(i * k - 1, 0), 0)),
                    pl.BlockSpec((8, d), lambda bi, i, gj: (0, 0))] + consts
        args = (x, x, prev, g, wup, wdw, wdn)
    else:
        assert t == tm
        mode, shift, ntail = "zero", 1, 8
        in_specs = [row] + consts
        args = (x, g, wup, wdw, wdn)
    out, tail = pl.pallas_call(
        functools.partial(_ffn_kernel, mode=mode, tm=tm, fc=fc, shift=shift, ngroups=ngroups),
        grid=(bsz, t // tm, ngroups),
        in_specs=in_specs,
        out_specs=[row, pl.BlockSpec((None, 2, ntail, fg), lambda bi, i, gj: (bi, 0, 0, gj))],
        out_shape=[jax.ShapeDtypeStruct((bsz, t, d), F32), jax.ShapeDtypeStruct((bsz, 2, ntail, f), F32)],
        compiler_params=_cparams(("parallel", "arbitrary", "arbitrary")),
        name="conv_ffn_" + mode,
    )(*args)
    return out, tail


def _wo_kernel(h_ref, o_ref, w_ref, out_ref):
    out_ref[...] = h_ref[...] + _dot(o_ref[...], w_ref[...])


def _wo_call(h, o, w, tm):
    bsz, t, d = h.shape
    hv = o.shape[-1]
    return pl.pallas_call(
        _wo_kernel,
        grid=(bsz, t // tm),
        in_specs=[pl.BlockSpec((None, tm, d), lambda bi, i: (bi, i, 0)),
                  pl.BlockSpec((None, tm, hv), lambda bi, i: (bi, i, 0)), _const((hv, d))],
        out_specs=pl.BlockSpec((None, tm, d), lambda bi, i: (bi, i, 0)),
        out_shape=jax.ShapeDtypeStruct((bsz, t, d), F32),
        compiler_params=_cparams(("parallel", "parallel")),
        name="attn_out_proj",
    )(h, o, w)


def _proj_kernel(h_ref, cs_ref, sn_ref, gkv_ref, gmix_ref, wkv_ref, glat_ref, gkr_ref, wdq_ref, gql_ref,
                 wqm_ref, wqr_ref, gq_ref, wuk_ref, gk_ref, wuv_ref, vone_ref,
                 c_ref, kr_ref, q_ref, k_ref, v_ref, *, nheads, nope, rope, scale):
    h = h_ref[...]
    hs = h * _rms_scale(h)
    hn = (hs * gkv_ref[...]).astype(BF16)
    xq = (hs * gmix_ref[...]).astype(BF16)
    z = _dot(hn, wkv_ref[...])
    cpre, a, ar = z[:, :LANES], z[:, LANES:2 * LANES], z[:, 2 * LANES:3 * LANES]
    c = cpre * _rms_scale(cpre) * glat_ref[...]
    c_ref[...] = c
    cs = cs_ref[...]
    sn = sn_ref[...]
    ra = lax.rsqrt(jnp.sum(a * a, axis=-1, keepdims=True) * (1.0 / rope) + NORM_EPS)
    kr = ra * (a * gkr_ref[...] * cs + ar * sn)
    kr_ref[...] = kr
    cb = c.astype(BF16)
    v_ref[...] = (_dot(cb, wuv_ref[...]) + vone_ref[...]).astype(BF16)
    kn = _dot(cb, wuk_ref[...])
    ql = _dot(xq, wdq_ref[...])
    ql = (ql * _rms_scale(ql) * gql_ref[...]).astype(BF16)
    qm = _dot(ql, wqm_ref[...])
    qr = _dot(ql, wqr_ref[...])
    lane = lax.broadcasted_iota(jnp.int32, (1, HEAD_SLOT), 1)
    mn = (lane < nope).astype(F32)
    mr = jnp.logical_and(lane >= nope, lane < nope + rope).astype(F32)
    gq = gq_ref[...]
    gk = gk_ref[...]
    for hd in range(nheads):
        sl = slice(hd * HEAD_SLOT, (hd + 1) * HEAD_SLOT)
        qh = qm[:, sl]
        sq = qh * qh
        rn = lax.rsqrt(jnp.sum(sq * mn, axis=-1, keepdims=True) * (1.0 / nope) + NORM_EPS)
        rr = lax.rsqrt(jnp.sum(sq * mr, axis=-1, keepdims=True) * (1.0 / rope) + NORM_EPS)
        qv = (qh * gq * cs + qr[:, sl] * sn) * (mn * rn + mr * rr) * scale
        q_ref[:, sl] = qv.astype(BF16)
        kh = kn[:, sl]
        rk = lax.rsqrt(jnp.sum(kh * kh, axis=-1, keepdims=True) * (1.0 / nope) + NORM_EPS)
        k_ref[:, sl] = (kh * rk * gk + kr).astype(BF16)


def _proj_call(h, cs, sn, w, tm, nheads, nope, rope, vdim):
    bsz, t, d = h.shape
    ql = w["wdq"].shape[1]
    hs = nheads * HEAD_SLOT
    row = lambda n: pl.BlockSpec((None, tm, n), lambda bi, i: (bi, i, 0))
    tab = pl.BlockSpec((tm, HEAD_SLOT), lambda bi, i: (i, 0))
    in_specs = [row(d), tab, tab, _const((1, d)), _const((1, d)), _const((d, 3 * LANES)), _const((1, LANES)),
                _const((1, HEAD_SLOT)), _const((d, ql)), _const((1, ql)), _const((ql, hs)), _const((ql, hs)),
                _const((1, HEAD_SLOT)), _const((LANES, hs)), _const((1, HEAD_SLOT)), _const((LANES, hs)),
                _const((1, hs))]
    return pl.pallas_call(
        functools.partial(_proj_kernel, nheads=nheads, nope=nope, rope=rope,
                          scale=LOG2E / math.sqrt(nope + rope)),
        grid=(bsz, t // tm),
        in_specs=in_specs,
        out_specs=[row(LANES), row(HEAD_SLOT), row(hs), row(hs), row(hs)],
        out_shape=[jax.ShapeDtypeStruct((bsz, t, LANES), F32), jax.ShapeDtypeStruct((bsz, t, HEAD_SLOT), F32),
                   jax.ShapeDtypeStruct((bsz, t, hs), BF16), jax.ShapeDtypeStruct((bsz, t, hs), BF16),
                   jax.ShapeDtypeStruct((bsz, t, hs), BF16)],
        compiler_params=_cparams(("parallel", "parallel")),
        name="mla_proj",
    )(h, cs, sn, w["gkv"], w["gmix"], w["wkv"], w["glat"], w["gkr"], w["wdq"], w["gql"], w["wqm"], w["wqr"],
      w["gq"], w["wuk"], w["gk"], w["wuv"], w["vone"])


def _attn_update(hd, r0, nr, q_ref, kb, vb, mask, m_ref, acc_ref, first):
    rows = slice(r0, r0 + nr)
    q = q_ref[rows, hd * HEAD_SLOT:(hd + 1) * HEAD_SLOT]
    s = _dot_nt(q, kb)
    if mask is not None:
        s = jnp.where(mask, s, NEG)
    nk = s.shape[1]
    m_cur = jnp.max(s, axis=1, keepdims=True)
    if first:
        m_new = jnp.broadcast_to(m_cur, (nr, LANES))
    else:
        m_prev = m_ref[hd, rows, :]
        m_new = jnp.maximum(m_prev, m_cur)
    if nk % LANES == 0:
        mb = jnp.concatenate([m_new] * (nk // LANES), axis=1) if nk > LANES else m_new
    else:
        mb = m_new[:, :nk]
    p = jnp.exp2((s - mb).astype(BF16))
    pv = _dot(p, vb)
    if first:
        acc_ref[hd, rows, :] = pv
    else:
        alpha = jnp.exp2(m_prev - m_new)
        acc_ref[hd, rows, :] = alpha * acc_ref[hd, rows, :] + pv
    m_ref[hd, rows, :] = m_new


def _attn_kernel(*refs, tq, tk, dk, has_prefix, vdim, hps):
    if has_prefix:
        q_ref, k_ref, v_ref, pk_ref, pv_ref, o_ref, m_ref, acc_ref = refs
    else:
        q_ref, k_ref, v_ref, o_ref, m_ref, acc_ref = refs
    i = pl.program_id(2)
    cols = lambda hd: slice(hd * HEAD_SLOT, (hd + 1) * HEAD_SLOT)
    if has_prefix:
        for hd in range(hps):
            _attn_update(hd, 0, tq, q_ref, pk_ref[:, cols(hd)], pv_ref[:, cols(hd)], None, m_ref, acc_ref, True)
        nfull = i * (tq // tk)

        def body(j, carry):
            k0 = pl.multiple_of(j * tk, tk)
            for hd in range(hps):
                _attn_update(hd, 0, tq, q_ref, k_ref[pl.ds(k0, tk), cols(hd)], v_ref[pl.ds(k0, tk), cols(hd)],
                             None, m_ref, acc_ref, False)
            return carry

        lax.fori_loop(0, nfull, body, 0)
    for jj in range(tq // dk):
        r0 = jj * dk
        nr = tq - r0
        k0 = pl.multiple_of(i * tq + r0, dk)
        row = lax.broadcasted_iota(jnp.int32, (nr, dk), 0)
        col = lax.broadcasted_iota(jnp.int32, (nr, dk), 1)
        mask = col <= row
        for hd in range(hps):
            _attn_update(hd, r0, nr, q_ref, k_ref[pl.ds(k0, dk), cols(hd)], v_ref[pl.ds(k0, dk), cols(hd)], mask,
                         m_ref, acc_ref, (not has_prefix) and jj == 0)
    lane = lax.broadcasted_iota(jnp.int32, (tq, LANES), 1)
    for pr in range(hps // 2):
        a0 = acc_ref[2 * pr]
        a1 = acc_ref[2 * pr + 1]
        o0 = a0 / pltpu.roll(a0, vdim, axis=1)
        o1 = pltpu.roll(a1, vdim, axis=1) / a1
        o_ref[:, pr * LANES:(pr + 1) * LANES] = jnp.where(lane < vdim, o0, o1).astype(o_ref.dtype)


def _attn_call(q, k, v, pk, pv, tq, tk, dk, nheads, vdim):
    bsz, t, _ = q.shape
    hps = ATTN_HEADS_PER_STEP if nheads % ATTN_HEADS_PER_STEP == 0 else 2
    assert 2 * vdim == HEAD_SLOT and nheads % hps == 0 and t % tq == 0 and tq % tk == 0 and tq % dk == 0
    has_prefix = pk is not None
    assert has_prefix or t == tq
    tile = pl.BlockSpec((None, tq, hps * HEAD_SLOT), lambda bi, hg, i: (bi, i, hg))
    seq = pl.BlockSpec((None, t, hps * HEAD_SLOT), lambda bi, hg, i: (bi, 0, hg))
    in_specs = [tile, seq, seq]
    args = [q, k, v]
    if has_prefix:
        pre = pl.BlockSpec((pk.shape[0], hps * HEAD_SLOT), lambda bi, hg, i: (0, hg))
        in_specs += [pre, pre]
        args += [pk, pv]
    return pl.pallas_call(
        functools.partial(_attn_kernel, tq=tq, tk=tk, dk=dk, has_prefix=has_prefix, vdim=vdim, hps=hps),
        grid=(bsz, nheads // hps, t // tq),
        in_specs=in_specs,
        out_specs=pl.BlockSpec((None, tq, hps * vdim), lambda bi, hg, i: (bi, i, hg)),
        out_shape=jax.ShapeDtypeStruct((bsz, t, nheads * vdim), BF16),
        scratch_shapes=[pltpu.VMEM((hps, tq, LANES), F32)] * 2,
        compiler_params=_cparams(("parallel", "parallel", "arbitrary")),
        name="mla_prompt_attn",
    )(*args)


def _qt_kernel(q_ref, w_ref, o_ref, *, nheads):
    for hd in range(nheads):
        o_ref[hd] = _dot(q_ref[:, hd * HEAD_SLOT:(hd + 1) * HEAD_SLOT], w_ref[hd]).astype(BF16)


def _qt_call(q, w, nheads):
    r = q.shape[0]
    return pl.pallas_call(
        functools.partial(_qt_kernel, nheads=nheads),
        out_shape=jax.ShapeDtypeStruct((nheads, r, LANES), BF16),
        compiler_params=pltpu.CompilerParams(vmem_limit_bytes=VMEM_LIMIT),
        name="mla_absorb_q",
    )(q, w)


def _uv_kernel(ol_ref, w_ref, o_ref, *, nheads):
    for hp in range(nheads // 2):
        r = (_dot(ol_ref[2 * hp].astype(BF16), w_ref[2 * hp])
             + _dot(ol_ref[2 * hp + 1].astype(BF16), w_ref[2 * hp + 1]))
        o_ref[:, hp * LANES:(hp + 1) * LANES] = r.astype(BF16)


def _uv_call(ol, w, nheads, vdim):
    r = ol.shape[1]
    return pl.pallas_call(
        functools.partial(_uv_kernel, nheads=nheads),
        out_shape=jax.ShapeDtypeStruct((r, nheads * vdim), BF16),
        compiler_params=pltpu.CompilerParams(vmem_limit_bytes=VMEM_LIMIT),
        name="mla_value_up",
    )(ol, w)


def _paged_scores(lhs_ref, qr_ref, cb, krt, nheads, nq, nope):
    n = cb.shape[0]
    r = _dot_nt(lhs_ref[...], cb)
    kt = r[:nheads * nope]
    sraw = r[nheads * nope:]
    ss = jnp.sum((kt * kt).reshape(nheads, nope, n), axis=1)
    rs = lax.rsqrt(ss * (1.0 / nope) + NORM_EPS)
    srope = _dot(qr_ref[...], krt)
    return (sraw.reshape(nq, nheads, n) * rs[None, :, :]).reshape(nq * nheads, n) + srope


def _paged_kernel(pt_ref, lat_hbm, krt_hbm, wukt_ref, qt_ref, qr_ref, cn_ref, krn_ref, o_ref,
                  latbuf, krbuf, sem, lhs_ref, m_ref, l_ref, acc_ref, *, npages, ch, sub, nheads, nq, nope):
    b = pl.program_id(0)
    nb = pl.num_programs(0)
    nc = npages // ch
    page = latbuf.shape[1] // ch
    nrow = nq * nheads

    def lat_copy(pid, slot, p):
        return pltpu.make_async_copy(lat_hbm.at[pid], latbuf.at[slot, pl.ds(p * page, page), :], sem.at[slot, 0])

    def krt_copy(pid, slot, p):
        return pltpu.make_async_copy(krt_hbm.at[pid], krbuf.at[slot, :, pl.ds(p * page, page)], sem.at[slot, 1])

    def start_chunk(bb, c, slot):
        for p in range(ch):
            pid = pt_ref[bb * npages + c * ch + p]
            lat_copy(pid, slot, p).start()
            krt_copy(pid, slot, p).start()

    def wait_chunk(slot):
        for p in range(ch):
            lat_copy(0, slot, p).wait()
            krt_copy(0, slot, p).wait()

    def update(s, cb):
        n = s.shape[1]
        m_prev = m_ref[...]
        m_new = jnp.maximum(m_prev, jnp.max(s, axis=1, keepdims=True))
        p = jnp.exp2(s - jnp.concatenate([m_new] * (n // LANES), axis=1))
        alpha = jnp.exp2(m_prev - m_new)
        l_ref[...] = alpha * l_ref[...] + jnp.sum(p, axis=1, keepdims=True)
        acc_ref[...] = alpha * acc_ref[...] + _dot(p.astype(BF16), cb)
        m_ref[...] = m_new

    def compute(slot):
        cbs, scores = [], []
        for sb in range(ch * page // sub):
            keys = slice(sb * sub, (sb + 1) * sub)
            cbs.append(latbuf[slot, keys, :].astype(BF16))
            krt = krbuf[slot, :, keys].astype(BF16)
            scores.append(_paged_scores(lhs_ref, qr_ref, cbs[-1], krt, nheads, nq, nope))
        update(jnp.concatenate(scores, axis=1), jnp.concatenate(cbs, axis=0))

    @pl.when(b == 0)
    def _():
        start_chunk(0, 0, 0)
        lhs_ref[0:nheads * nope, :] = wukt_ref[...]

    lhs_ref[nheads * nope:, :] = qt_ref[...]

    cb = cn_ref[...].astype(BF16)
    s = _paged_scores(lhs_ref, qr_ref, cb, krn_ref[...].astype(BF16), nheads, nq, nope)
    n = s.shape[1]
    row = lax.broadcasted_iota(jnp.int32, (nrow, n), 0)
    col = lax.broadcasted_iota(jnp.int32, (nrow, n), 1)
    s = jnp.where(col * nheads <= row, s, NEG)
    m = jnp.max(s, axis=1, keepdims=True)
    p = jnp.exp2(s - m)
    m_ref[...] = jnp.broadcast_to(m, (nrow, LANES))
    l_ref[...] = jnp.broadcast_to(jnp.sum(p, axis=1, keepdims=True), (nrow, LANES))
    acc_ref[...] = _dot(p.astype(BF16), cb)

    def pair_body(cp, carry):
        c0 = 2 * cp
        start_chunk(b, c0 + 1, 1)
        wait_chunk(0)
        compute(0)

        @pl.when(c0 + 2 < nc)
        def _():
            start_chunk(b, c0 + 2, 0)

        @pl.when(jnp.logical_and(c0 + 2 == nc, b + 1 < nb))
        def _():
            start_chunk(b + 1, 0, 0)

        wait_chunk(1)
        compute(1)
        return carry

    lax.fori_loop(0, nc // 2, pair_body, 0)
    o_ref[...] = acc_ref[...] / l_ref[...]


def _paged_call(page_table, cache_lat, cache_krt, wukt, qt, qr, cn, krn, nheads, nq, nope):
    db, npages = page_table.shape
    _, page, clat = cache_lat.shape
    rdim = cache_krt.shape[1]
    ch = PAGES_PER_CHUNK
    while npages % (2 * ch):
        ch //= 2
    sub = min(PAGED_SUB_KEYS, ch * page)
    assert ch >= 1 and (ch * page) % sub == 0
    nrow = nq * nheads
    per_b = lambda shape: pl.BlockSpec((None,) + shape, lambda b, pt: (b, 0, 0))
    in_specs = [pl.BlockSpec(memory_space=pl.ANY), pl.BlockSpec(memory_space=pl.ANY),
                pl.BlockSpec(wukt.shape, lambda b, pt: (0, 0)),
                per_b((nrow, clat)), per_b((nrow, rdim)), per_b(cn.shape[1:]), per_b(krn.shape[1:])]
    return pl.pallas_call(
        functools.partial(_paged_kernel, npages=npages, ch=ch, sub=sub, nheads=nheads, nq=nq, nope=nope),
        grid_spec=pltpu.PrefetchScalarGridSpec(
            num_scalar_prefetch=1,
            grid=(db,),
            in_specs=in_specs,
            out_specs=per_b((nrow, clat)),
            scratch_shapes=[pltpu.VMEM((2, ch * page, clat), F32), pltpu.VMEM((2, rdim, ch * page), F32),
                            pltpu.SemaphoreType.DMA((2, 2)), pltpu.VMEM((wukt.shape[0] + nrow, clat), BF16)]
            + [pltpu.VMEM((nrow, LANES), F32)] * 3),
        out_shape=jax.ShapeDtypeStruct((db, nrow, clat), F32),
        compiler_params=_cparams(("arbitrary",)),
        name="mla_paged_attn",
    )(page_table.reshape(-1), cache_lat, cache_krt, wukt, qt, qr, cn, krn)


def _slot(x, lo, width=HEAD_SLOT):
    n = x.shape[-1]
    pad = [(0, 0)] * (x.ndim - 1) + [(lo, width - lo - n)]
    return jnp.pad(x, pad)


def _half_rot_cols(w):
    half = w.shape[-1] // 2
    return jnp.concatenate([-w[..., half:], w[..., :half]], axis=-1)


def _rope_tables(pos, rope, nope):
    half = rope // 2
    inv = ROPE_THETA ** (-jnp.arange(half, dtype=F32) / half)
    ang = pos.astype(F32)[:, None] * inv[None, :]
    cos, sin = jnp.cos(ang), jnp.sin(ang)
    cs = jnp.concatenate([jnp.ones((pos.shape[0], nope), F32), cos, cos], axis=1)
    sn = jnp.concatenate([jnp.zeros((pos.shape[0], nope), F32), sin, sin], axis=1)
    return _slot(cs, 0), _slot(sn, 0)


def kernel(x_prompt, x_sample, state_conv_a, state_ffn_conv, cache_kv_latent, cache_k_rope, page_table, meta_tokens, norm_mix, norm_ffn, a_w_pw1, a_b_pw1, a_w_dw, a_b_dw, a_ln_g, a_ln_b, a_w_pw2, a_b_pw2, ffn_w_up, ffn_w_dw, ffn_w_down, kv_norm, mla_w_dkv, mla_lat_norm, mla_w_kr, mla_knorm_rope, mla_w_uk, mla_w_uv, mla_knorm_nope, mla_w_dq, mla_q_lat_norm, mla_w_uq, mla_qnorm_nope, mla_qnorm_rope, mla_w_o):
    bsz, seq, d = x_prompt.shape
    db, nq, _ = x_sample.shape
    nmeta = meta_tokens.shape[0]
    depth = norm_mix.shape[0]
    n_a = a_w_pw1.shape[0]
    n_b = mla_w_dq.shape[0]
    f = ffn_w_down.shape[1]
    clat, nheads, nope = mla_w_uk.shape
    rope = mla_w_kr.shape[1]
    vdim = mla_w_uv.shape[2]
    width_a = a_w_dw.shape[1]
    npages, page = page_table.shape[1], cache_kv_latent.shape[1]
    past_len = npages * page
    assert n_b == 1 and depth == n_a + n_b and clat == LANES and nope + rope <= HEAD_SLOT
    assert seq >= width_a - 1 and nmeta % 16 == 0 and nmeta >= 8
    tm = ROW_TILE if seq % ROW_TILE == 0 else seq
    rs = nq * db
    row2 = lambda v: v.reshape(1, -1)

    hp = x_prompt
    hm = meta_tokens.astype(F32)[None]
    hs = jnp.transpose(x_sample, (1, 0, 2)).reshape(1, rs, d)

    conv_a_p, conv_a_s, ffn_p, ffn_s = [], [], [], []

    def run_ffn(layer, hp, hm, hs):
        g = row2(norm_ffn[layer])
        wup = jnp.transpose(ffn_w_up[layer].reshape(d, 2, f), (1, 0, 2)).astype(BF16)
        wdw = jnp.transpose(ffn_w_dw[layer].reshape(3, 2, f), (1, 0, 2))
        wdn = ffn_w_down[layer].astype(BF16)
        st = jnp.transpose(state_ffn_conv[layer].reshape(db, 2, 2, f), (2, 1, 0, 3))
        hm_new, _ = _ffn_call(hm, g, wup, wdw, wdn, nmeta)
        hp_new, tail_p = _ffn_call(hp, g, wup, wdw, wdn, tm, prev=hm[0, nmeta - 8:])
        hs_new, tail_s = _ffn_call(hs, g, wup, wdw, wdn, rs, state=st)
        ffn_p.append(jnp.transpose(tail_p[:, :, 6:8, :], (0, 2, 1, 3)).reshape(bsz, 2, 2 * f))
        ffn_s.append(jnp.transpose(tail_s[0].reshape(2, 2, db, f), (2, 1, 0, 3)).reshape(db, 2, 2 * f))
        return hp_new, hm_new, hs_new

    for layer in range(n_a):
        g = row2(norm_mix[layer])
        w1 = a_w_pw1[layer].astype(BF16)
        b1 = row2(a_b_pw1[layer])
        conv_w = (a_w_dw[layer], row2(a_b_dw[layer]), row2(a_ln_g[layer]), row2(a_ln_b[layer]),
                  a_w_pw2[layer].astype(BF16), row2(a_b_pw2[layer]))
        um = _a1_call(hm, g, w1, b1, nmeta)
        up = _a1_call(hp, g, w1, b1, tm)
        us = _a1_call(hs, g, w1, b1, rs)
        halo = jnp.pad(um[0], ((32 - nmeta, 0), (0, 0))) if nmeta < 32 else um[0, nmeta - 32:]
        hm_new = _a2_seq_call(um, hm, None, *conv_w, nmeta)
        hp = _a2_seq_call(up, hp, halo, *conv_w, tm)
        hm = hm_new
        xp_s = jnp.concatenate([jnp.transpose(state_conv_a[layer], (1, 0, 2)), us.reshape(nq, db, d)], axis=0)
        hs = _a2_tm_call(xp_s, hs.reshape(nq, db, d), *conv_w, 32 if db % 32 == 0 else db).reshape(1, rs, d)
        conv_a_p.append(up[:, seq - (width_a - 1):])
        conv_a_s.append(jnp.transpose(xp_s[nq:], (1, 0, 2)))
        hp, hm, hs = run_ffn(layer, hp, hm, hs)

    layer = n_a
    g_kr = mla_knorm_rope
    wkr_g = mla_w_kr * g_kr[None, :]
    uq = mla_w_uq[0]
    uq_rope_g = uq[:, :, nope:] * mla_qnorm_rope[0][None, None, :]
    pw = {
        "gkv": row2(kv_norm), "gmix": row2(norm_mix[layer]),
        "wkv": jnp.concatenate([mla_w_dkv, _slot(mla_w_kr, nope), _slot(_half_rot_cols(wkr_g), nope)], axis=1).astype(BF16),
        "glat": row2(mla_lat_norm), "gkr": _slot(row2(g_kr), nope),
        "wdq": mla_w_dq[0].astype(BF16), "gql": row2(mla_q_lat_norm[0]),
        "wqm": _slot(uq, 0).reshape(uq.shape[0], nheads * HEAD_SLOT).astype(BF16),
        "wqr": _slot(_half_rot_cols(uq_rope_g), nope).reshape(uq.shape[0], nheads * HEAD_SLOT).astype(BF16),
        "gq": _slot(row2(jnp.concatenate([mla_qnorm_nope[0], mla_qnorm_rope[0]])), 0),
        "wuk": _slot(mla_w_uk, 0).reshape(clat, nheads * HEAD_SLOT).astype(BF16),
        "gk": _slot(row2(mla_knorm_nope), 0),
        "wuv": _slot(mla_w_uv, 0).reshape(clat, nheads * HEAD_SLOT).astype(BF16),
        "vone": jnp.tile(_slot(jnp.ones((1, HEAD_SLOT - vdim), F32), vdim), (1, nheads)),
    }
    cs_m, sn_m = _rope_tables(jnp.arange(nmeta), rope, nope)
    cs_p, sn_p = _rope_tables(nmeta + jnp.arange(seq), rope, nope)
    cs_s, sn_s = _rope_tables(jnp.repeat(past_len + jnp.arange(nq), db), rope, nope)
    proj = functools.partial(_proj_call, w=pw, nheads=nheads, nope=nope, rope=rope, vdim=vdim)
    c_m, kr_m, q_m, k_m, v_m = proj(hm, cs_m, sn_m, tm=nmeta)
    c_p, kr_p, q_p, k_p, v_p = proj(hp, cs_p, sn_p, tm=tm)
    c_s, kr_s, q_s, _, _ = proj(hs, cs_s, sn_s, tm=rs)
    kr_m, kr_p, kr_s = (x[..., nope:nope + rope] for x in (kr_m, kr_p, kr_s))

    tq = ATTN_Q_TILE if seq % ATTN_Q_TILE == 0 else seq
    tk = ATTN_K_TILE if tq % ATTN_K_TILE == 0 else tq
    dk = ATTN_DIAG_TILE if tq % ATTN_DIAG_TILE == 0 else tq
    o_m = _attn_call(q_m, k_m, v_m, None, None, nmeta, nmeta, nmeta, nheads, vdim)
    o_p = _attn_call(q_p, k_p, v_p, k_m[0], v_m[0], tq, tk, dk, nheads, vdim)

    wukt = jnp.transpose(mla_w_uk.reshape(clat, nheads * nope)).astype(BF16)
    wqt = jnp.transpose(mla_w_uk, (1, 2, 0)) * mla_knorm_nope[None, :, None]
    wqt = jnp.pad(wqt, ((0, 0), (0, HEAD_SLOT - nope), (0, 0))).astype(BF16)
    qt = _qt_call(q_s[0], wqt, nheads)
    qt = jnp.transpose(qt.reshape(nheads, nq, db, clat), (2, 1, 0, 3)).reshape(db, nq * nheads, clat)
    qr = jnp.transpose(q_s[0].reshape(nq, db, nheads, HEAD_SLOT)[..., nope:nope + rope], (1, 0, 2, 3))
    qr = qr.reshape(db, nq * nheads, rope)
    nnew = LANES
    cn = jnp.pad(jnp.transpose(c_s[0].reshape(nq, db, clat), (1, 0, 2)), ((0, 0), (0, nnew - nq), (0, 0)))
    krn = jnp.pad(jnp.transpose(kr_s[0].reshape(nq, db, rope), (1, 2, 0)), ((0, 0), (0, 0), (0, nnew - nq)))
    cache_krt = jnp.transpose(cache_k_rope, (0, 2, 1))
    o_lat = _paged_call(page_table, cache_kv_latent, cache_krt, wukt, qt, qr, cn, krn, nheads, nq, nope)
    ol = jnp.transpose(o_lat.reshape(db, nq, nheads, clat), (2, 1, 0, 3)).reshape(nheads, rs, clat)
    wuv_pad = jnp.stack([_slot(mla_w_uv[:, hd, :], (hd % 2) * vdim, LANES) for hd in range(nheads)]).astype(BF16)
    o_s = _uv_call(ol, wuv_pad, nheads, vdim)[None]

    wo = mla_w_o[0].astype(BF16)
    hm = _wo_call(hm, o_m, wo, nmeta)
    hp = _wo_call(hp, o_p, wo, tm)
    hs = _wo_call(hs, o_s, wo, rs)
    hp, hm, hs = run_ffn(layer, hp, hm, hs)

    y_sample = jnp.transpose(hs.reshape(nq, db, d), (1, 0, 2))
    kv_lat_p = jnp.concatenate([jnp.broadcast_to(c_m, (bsz, nmeta, clat)), c_p], axis=1)
    k_rope_p = jnp.concatenate([jnp.broadcast_to(kr_m, (bsz, nmeta, rope)), kr_p], axis=1)
    kv_lat_s = jnp.transpose(c_s[0].reshape(nq, db, clat), (1, 0, 2))
    k_rope_s = jnp.transpose(kr_s[0].reshape(nq, db, rope), (1, 0, 2))
    return (hp, y_sample, jnp.stack(conv_a_p), jnp.stack(conv_a_s), jnp.stack(ffn_p), jnp.stack(ffn_s),
            kv_lat_p, k_rope_p, kv_lat_s, k_rope_s)
```

```python
import functools
import math

import jax
import jax.numpy as jnp
from jax import lax
from jax.experimental import pallas as pl
from jax.experimental.pallas import tpu as pltpu

F32 = jnp.float32
BF16 = jnp.bfloat16
NORM_EPS = 1e-6
ROPE_THETA = 10000.0
NEG = -1e30

LANES = 128
HEAD_SLOT = 128
ROW_TILE = 512
ATTN_Q_TILE = 1024
ATTN_K_TILE = 512
ATTN_DIAG_TILE = 512
FFN_CHUNK = 256
FFN_SEQ_CHUNK = 2816
ATTN_HEADS_PER_STEP = 4
PAGED_SUB_KEYS = 2048
LOG2E = 1.4426950408889634
VMEM_LIMIT = 56 * 1024 * 1024


def _cparams(sem):
    return pltpu.CompilerParams(dimension_semantics=sem, vmem_limit_bytes=VMEM_LIMIT)


def _const(shape):
    zeros = (0,) * len(shape)
    return pl.BlockSpec(shape, lambda *_: zeros)


def _sigmoid(x):
    return 1.0 / (1.0 + jnp.exp(-x))


def _rms_scale(x):
    return lax.rsqrt(jnp.mean(x * x, axis=-1, keepdims=True) + NORM_EPS)


def _dot(a, b):
    return jnp.dot(a, b, preferred_element_type=F32)


def _dot_nt(a, b):
    return lax.dot_general(a, b, (((1,), (1,)), ((), ())), preferred_element_type=F32)


def _a1_kernel(x_ref, g_ref, w_ref, b_ref, u_ref):
    x = x_ref[...]
    xn = x * _rms_scale(x) * g_ref[...]
    z = _dot(xn.astype(BF16), w_ref[...]) + b_ref[...]
    d = u_ref.shape[-1]
    u_ref[...] = z[:, :d] * _sigmoid(z[:, d:])


def _a1_call(x, g, w, b, tm):
    bsz, t, d = x.shape
    return pl.pallas_call(
        _a1_kernel,
        grid=(bsz, t // tm),
        in_specs=[pl.BlockSpec((None, tm, d), lambda bi, i: (bi, i, 0)),
                  _const((1, d)), _const((d, 2 * d)), _const((1, 2 * d))],
        out_specs=pl.BlockSpec((None, tm, d), lambda bi, i: (bi, i, 0)),
        out_shape=jax.ShapeDtypeStruct((bsz, t, d), F32),
        compiler_params=_cparams(("parallel", "parallel")),
        name="conv_pw1_glu",
    )(x, g, w, b)


def _ln_swish_pw2(y, h, lng_ref, lnb_ref, w2_ref, b2_ref):
    mu = jnp.mean(y, axis=-1, keepdims=True)
    yc = y - mu
    var = jnp.mean(yc * yc, axis=-1, keepdims=True)
    yn = yc * lax.rsqrt(var + NORM_EPS) * lng_ref[...] + lnb_ref[...]
    a = yn * _sigmoid(yn)
    return h + _dot(a.astype(BF16), w2_ref[...]) + b2_ref[...]


def _a2_seq_kernel(*refs, tm, halo, width, has_prev):
    if has_prev:
        (u_ref, uh_ref, prev_ref, h_ref, wdw_ref, bdw_ref, lng_ref, lnb_ref, w2_ref, b2_ref,
         o_ref, ext_ref, y_ref) = refs
        i = pl.program_id(1)

        @pl.when(i == 0)
        def _():
            ext_ref[0:halo, :] = prev_ref[...]

        @pl.when(i > 0)
        def _():
            ext_ref[0:halo, :] = uh_ref[...]
    else:
        (u_ref, h_ref, wdw_ref, bdw_ref, lng_ref, lnb_ref, w2_ref, b2_ref,
         o_ref, ext_ref, y_ref) = refs
        ext_ref[0:halo, :] = jnp.zeros((halo, ext_ref.shape[1]), F32)
    ext_ref[halo:halo + tm, :] = u_ref[...]
    d = u_ref.shape[-1]
    off = halo - (width - 1)

    taps = {}
    for k in range(width):
        taps.setdefault((off + k) % 8, []).append(((off + k) // 8, k))
    nblk = tm // 8

    def col_body(c, carry):
        c0 = pl.multiple_of(c * LANES, LANES)
        cols = pl.ds(c0, LANES)
        bias = jnp.broadcast_to(bdw_ref[:, cols], (8, LANES))
        sub = lax.broadcasted_iota(jnp.int32, (8, LANES), 0)
        xs = {}

        def xblk(jb):
            if jb not in xs:
                xs[jb] = ext_ref[pl.ds(jb * 8, 8), cols]
            return xs[jb]

        def zsum(r, jb):
            acc = None
            for a, k in taps[r]:
                term = wdw_ref[pl.ds(k * 8, 8), cols] * xblk(jb + a)
                acc = term if acc is None else acc + term
            return acc

        zprev = {r: zsum(r, 0) for r in taps if r != 0}
        for jb in range(nblk):
            y = zsum(0, jb) + bias if 0 in taps else bias
            znext = {}
            for r in zprev:
                znext[r] = zsum(r, jb + 1)
                y = y + pltpu.roll(jnp.where(sub >= r, zprev[r], znext[r]), 8 - r, axis=0)
            y_ref[pl.ds(jb * 8, 8), cols] = y
            zprev = znext
        return carry

    lax.fori_loop(0, d // LANES, col_body, 0)
    o_ref[...] = _ln_swish_pw2(y_ref[...], h_ref[...], lng_ref, lnb_ref, w2_ref, b2_ref)


def _a2_seq_call(u, h, prev, wdw, bdw, lng, lnb, w2, b2, tm):
    bsz, t, d = u.shape
    width = wdw.shape[0]
    halo = 32
    assert halo >= width - 1 and halo % 8 == 0 and (tm % halo == 0 or prev is None)
    wdw = jnp.repeat(wdw, 8, axis=0)
    row = pl.BlockSpec((None, tm, d), lambda bi, i: (bi, i, 0))
    consts = [_const((width * 8, d)), _const((1, d)), _const((1, d)), _const((1, d)), _const((d, d)), _const((1, d))]
    if prev is not None:
        k = tm // halo
        in_specs = [row, pl.BlockSpec((None, halo, d), lambda bi, i: (bi, jnp.maximum(i * k - 1, 0), 0)),
                    _const((halo, d)), row] + consts
        args = (u, u, prev, h, wdw, bdw, lng, lnb, w2, b2)
    else:
        assert t == tm
        in_specs = [row, row] + consts
        args = (u, h, wdw, bdw, lng, lnb, w2, b2)
    return pl.pallas_call(
        functools.partial(_a2_seq_kernel, tm=tm, halo=halo, width=width, has_prev=prev is not None),
        grid=(bsz, t // tm),
        in_specs=in_specs,
        out_specs=row,
        out_shape=jax.ShapeDtypeStruct((bsz, t, d), F32),
        scratch_shapes=[pltpu.VMEM((halo + tm, d), F32), pltpu.VMEM((tm, d), F32)],
        compiler_params=_cparams(("parallel", "parallel")),
        name="conv_dw_ln_pw2",
    )(*args)


def _a2_tm_kernel(xp_ref, h_ref, wdw_ref, bdw_ref, lng_ref, lnb_ref, w2_ref, b2_ref, o_ref, y_ref, *, width, cw):
    nt, bs, d = h_ref.shape
    for c in range(d // cw):
        cols = slice(c * cw, (c + 1) * cw)
        wk = [jnp.broadcast_to(wdw_ref[k:k + 1, cols], (bs, cw)) for k in range(width)]
        bias = bdw_ref[:, cols]
        for t in range(nt):
            acc = wk[0] * xp_ref[t, :, cols]
            for k in range(1, width):
                acc = acc + wk[k] * xp_ref[t + k, :, cols]
            y_ref[t * bs:(t + 1) * bs, cols] = acc + bias
    h = h_ref[...].reshape(nt * bs, d)
    out = _ln_swish_pw2(y_ref[...], h, lng_ref, lnb_ref, w2_ref, b2_ref)
    o_ref[...] = out.reshape(nt, bs, d)


def _a2_tm_call(xp, h, wdw, bdw, lng, lnb, w2, b2, bs):
    nt, db, d = h.shape
    width = wdw.shape[0]
    return pl.pallas_call(
        functools.partial(_a2_tm_kernel, width=width, cw=2 * LANES),
        grid=(db // bs,),
        in_specs=[pl.BlockSpec((nt + width - 1, bs, d), lambda i: (0, i, 0)),
                  pl.BlockSpec((nt, bs, d), lambda i: (0, i, 0)),
                  _const((width, d)), _const((1, d)), _const((1, d)), _const((1, d)), _const((d, d)), _const((1, d))],
        out_specs=pl.BlockSpec((nt, bs, d), lambda i: (0, i, 0)),
        out_shape=jax.ShapeDtypeStruct((nt, db, d), F32),
        scratch_shapes=[pltpu.VMEM((nt * bs, d), F32)],
        compiler_params=_cparams(("parallel",)),
        name="conv_dw_ln_pw2_sample",
    )(xp, h, wdw, bdw, lng, lnb, w2, b2)


def _ffn_kernel(*refs, mode, tm, fc, shift, ngroups):
    if mode == "prev":
        x_ref, xh_ref, prev_ref, g_ref, wg_ref, wu_ref, wdw_ref, wdn_ref, o_ref, tail_ref = refs
    elif mode == "zero":
        x_ref, g_ref, wg_ref, wu_ref, wdw_ref, wdn_ref, o_ref, tail_ref = refs
    else:
        x_ref, st_ref, g_ref, wg_ref, wu_ref, wdw_ref, wdn_ref, o_ref, tail_ref = refs
    x = x_ref[...]
    d = x.shape[-1]
    f = wdn_ref.shape[0]
    if mode == "prev":
        first = pl.program_id(1) == 0
        halo = jnp.where(first, prev_ref[...], xh_ref[...])
        xe = jnp.concatenate([halo, x], axis=0)
    elif mode == "zero":
        xe = jnp.concatenate([jnp.zeros((8, d), F32), x], axis=0)
    else:
        xe = x
    pre = xe.shape[0] - tm if mode != "state" else 2 * shift
    xn = (xe * _rms_scale(xe) * g_ref[...]).astype(BF16)
    ntail = tail_ref.shape[1]
    fg = wdn_ref.shape[0]
    acc = jnp.zeros((tm, d), F32)
    for j in range(fg // fc):
        cols = slice(j * fc, (j + 1) * fc)
        act = None
        for gu in range(2):
            hu = _dot(xn, (wg_ref, wu_ref)[gu][:, cols])
            tail_ref[gu, :, cols] = hu[hu.shape[0] - ntail:, :]
            if mode == "state":
                hu = jnp.concatenate([st_ref[gu, 0, :, cols], st_ref[gu, 1, :, cols], hu], axis=0)
            w = wdw_ref[gu, :, cols]
            conv = (w[0:1] * hu[pre - 2 * shift:pre - 2 * shift + tm]
                    + w[1:2] * hu[pre - shift:pre - shift + tm]
                    + w[2:3] * hu[pre:pre + tm])
            act = conv * _sigmoid(conv) if gu == 0 else act * conv
        acc = acc + _dot(act.astype(BF16), wdn_ref[cols, :])
    if ngroups == 1:
        o_ref[...] = x + acc
    else:
        gj = pl.program_id(2)

        @pl.when(gj == 0)
        def _():
            o_ref[...] = x + acc

        @pl.when(gj > 0)
        def _():
            o_ref[...] = o_ref[...] + acc


def _ffn_call(x, g, wup, wdw, wdn, tm, prev=None, state=None):
    bsz, t, d = x.shape
    f = wdn.shape[0]
    fc = FFN_CHUNK if f % FFN_CHUNK == 0 else LANES
    if state is None and f % FFN_SEQ_CHUNK == 0:
        fc = FFN_SEQ_CHUNK
    fg = fc if state is not None else f
    ngroups = f // fg
    row = pl.BlockSpec((None, tm, d), lambda bi, i, gj: (bi, i, 0))
    consts = [pl.BlockSpec((1, d), lambda bi, i, gj: (0, 0)),
              pl.BlockSpec((d, fg), lambda bi, i, gj: (0, gj)),
              pl.BlockSpec((d, fg), lambda bi, i, gj: (0, ngroups + gj)),
              pl.BlockSpec((2, 3, fg), lambda bi, i, gj: (0, 0, gj)),
              pl.BlockSpec((fg, d), lambda bi, i, gj: (gj, 0))]
    if state is not None:
        assert bsz == 1 and t == tm
        mode, shift, ntail = "state", state.shape[2], 2 * state.shape[2]
        in_specs = [row, pl.BlockSpec(state.shape[:3] + (fg,), lambda bi, i, gj: (0, 0, 0, gj))] + consts
        args = (x, state, g, wup, wup, wdw, wdn)
    elif prev is not None:
        mode, shift, ntail = "prev", 1, 8
        k = tm // 8
        in_specs = [row, pl.BlockSpec((None, 8, d), lambda bi, i, gj: (bi, jnp.maximum(i * k - 1, 0), 0)),
                    pl.BlockSpec((8, d), lambda bi, i, gj: (0, 0))] + consts
        args = (x, x, prev, g, wup, wup, wdw, wdn)
    else:
        assert t == tm
        mode, shift, ntail = "zero", 1, 8
        in_specs = [row] + consts
        args = (x, g, wup, wup, wdw, wdn)
    out, tail = pl.pallas_call(
        functools.partial(_ffn_kernel, mode=mode, tm=tm, fc=fc, shift=shift, ngroups=ngroups),
        grid=(bsz, t // tm, ngroups),
        in_specs=in_specs,
        out_specs=[row, pl.BlockSpec((None, 2, ntail, fg), lambda bi, i, gj: (bi, 0, 0, gj))],
        out_shape=[jax.ShapeDtypeStruct((bsz, t, d), F32), jax.ShapeDtypeStruct((bsz, 2, ntail, f), F32)],
        compiler_params=_cparams(("parallel", "arbitrary", "arbitrary")),
        name="conv_ffn_" + mode,
    )(*args)
    return out, tail


def _wo_kernel(h_ref, o_ref, w_ref, out_ref):
    out_ref[...] = h_ref[...] + _dot(o_ref[...], w_ref[...])


def _wo_call(h, o, w, tm):
    bsz, t, d = h.shape
    hv = o.shape[-1]
    return pl.pallas_call(
        _wo_kernel,
        grid=(bsz, t // tm),
        in_specs=[pl.BlockSpec((None, tm, d), lambda bi, i: (bi, i, 0)),
                  pl.BlockSpec((None, tm, hv), lambda bi, i: (bi, i, 0)), _const((hv, d))],
        out_specs=pl.BlockSpec((None, tm, d), lambda bi, i: (bi, i, 0)),
        out_shape=jax.ShapeDtypeStruct((bsz, t, d), F32),
        compiler_params=_cparams(("parallel", "parallel")),
        name="attn_out_proj",
    )(h, o, w)


def _proj_kernel(h_ref, cs_ref, sn_ref, gkv_ref, gmix_ref, wkv_ref, glat_ref, gkr_ref, wdq_ref, gql_ref,
                 wqm_ref, wqr_ref, gq_ref, wuk_ref, gk_ref, wuv_ref, vone_ref,
                 c_ref, kr_ref, q_ref, k_ref, v_ref, *, nheads, nope, rope, scale):
    h = h_ref[...]
    hs = h * _rms_scale(h)
    hn = (hs * gkv_ref[...]).astype(BF16)
    xq = (hs * gmix_ref[...]).astype(BF16)
    z = _dot(hn, wkv_ref[...])
    cpre, a, ar = z[:, :LANES], z[:, LANES:2 * LANES], z[:, 2 * LANES:3 * LANES]
    c = cpre * _rms_scale(cpre) * glat_ref[...]
    c_ref[...] = c
    cs = cs_ref[...]
    sn = sn_ref[...]
    ra = lax.rsqrt(jnp.sum(a * a, axis=-1, keepdims=True) * (1.0 / rope) + NORM_EPS)
    kr = ra * (a * gkr_ref[...] * cs + ar * sn)
    kr_ref[...] = kr
    cb = c.astype(BF16)
    v_ref[...] = (_dot(cb, wuv_ref[...]) + vone_ref[...]).astype(BF16)
    kn = _dot(cb, wuk_ref[...])
    ql = _dot(xq, wdq_ref[...])
    ql = (ql * _rms_scale(ql) * gql_ref[...]).astype(BF16)
    qm = _dot(ql, wqm_ref[...])
    qr = _dot(ql, wqr_ref[...])
    lane = lax.broadcasted_iota(jnp.int32, (1, HEAD_SLOT), 1)
    mn = (lane < nope).astype(F32)
    mr = jnp.logical_and(lane >= nope, lane < nope + rope).astype(F32)
    csq = cs * (gq_ref[...] * scale)
    snq = sn * scale
    gk = gk_ref[...]
    for hd in range(nheads):
        sl = slice(hd * HEAD_SLOT, (hd + 1) * HEAD_SLOT)
        qh = qm[:, sl]
        sq = qh * qh
        rn = lax.rsqrt(jnp.sum(sq * mn, axis=-1, keepdims=True) * (1.0 / nope) + NORM_EPS)
        rr = lax.rsqrt(jnp.sum(sq * mr, axis=-1, keepdims=True) * (1.0 / rope) + NORM_EPS)
        qv = (qh * csq + qr[:, sl] * snq) * jnp.where(lane < nope, rn, rr)
        q_ref[:, sl] = qv.astype(BF16)
        kh = kn[:, sl]
        rk = lax.rsqrt(jnp.sum(kh * kh, axis=-1, keepdims=True) * (1.0 / nope) + NORM_EPS)
        k_ref[:, sl] = (kh * rk * gk + kr).astype(BF16)


def _proj_call(h, cs, sn, w, tm, nheads, nope, rope):
    bsz, t, d = h.shape
    ql = w["wdq"].shape[1]
    hs = nheads * HEAD_SLOT
    row = lambda n: pl.BlockSpec((None, tm, n), lambda bi, i: (bi, i, 0))
    tab = pl.BlockSpec((tm, HEAD_SLOT), lambda bi, i: (i, 0))
    in_specs = [row(d), tab, tab, _const((1, d)), _const((1, d)), _const((d, 3 * LANES)), _const((1, LANES)),
                _const((1, HEAD_SLOT)), _const((d, ql)), _const((1, ql)), _const((ql, hs)), _const((ql, hs)),
                _const((1, HEAD_SLOT)), _const((LANES, hs)), _const((1, HEAD_SLOT)), _const((LANES, hs)),
                _const((1, hs))]
    return pl.pallas_call(
        functools.partial(_proj_kernel, nheads=nheads, nope=nope, rope=rope,
                          scale=LOG2E / math.sqrt(nope + rope)),
        grid=(bsz, t // tm),
        in_specs=in_specs,
        out_specs=[row(LANES), row(HEAD_SLOT), row(hs), row(hs), row(hs)],
        out_shape=[jax.ShapeDtypeStruct((bsz, t, LANES), F32), jax.ShapeDtypeStruct((bsz, t, HEAD_SLOT), F32),
                   jax.ShapeDtypeStruct((bsz, t, hs), BF16), jax.ShapeDtypeStruct((bsz, t, hs), BF16),
                   jax.ShapeDtypeStruct((bsz, t, hs), BF16)],
        compiler_params=_cparams(("parallel", "parallel")),
        name="mla_proj",
    )(h, cs, sn, w["gkv"], w["gmix"], w["wkv"], w["glat"], w["gkr"], w["wdq"], w["gql"], w["wqm"], w["wqr"],
      w["gq"], w["wuk"], w["gk"], w["wuv"], w["vone"])


def _attn_update(hd, r0, nr, q_ref, kb, vb, mask, m_ref, acc_ref, first):
    rows = slice(r0, r0 + nr)
    q = q_ref[rows, hd * HEAD_SLOT:(hd + 1) * HEAD_SLOT]
    s = _dot_nt(q, kb)
    if mask is not None:
        s = jnp.where(mask, s, NEG)
    nk = s.shape[1]
    m_cur = jnp.max(s, axis=1, keepdims=True)
    if first:
        m_new = jnp.broadcast_to(m_cur, (nr, LANES))
    else:
        m_prev = m_ref[hd, rows, :]
        m_new = jnp.maximum(m_prev, m_cur)
    if nk % LANES == 0:
        mb = jnp.concatenate([m_new] * (nk // LANES), axis=1) if nk > LANES else m_new
    else:
        mb = m_new[:, :nk]
    p = jnp.exp2((s - mb).astype(BF16))
    pv = _dot(p, vb)
    if first:
        acc_ref[hd, rows, :] = pv
    else:
        alpha = jnp.exp2(m_prev - m_new)
        acc_ref[hd, rows, :] = alpha * acc_ref[hd, rows, :] + pv
    m_ref[hd, rows, :] = m_new


def _attn_kernel(*refs, tq, tk, dk, has_prefix, vdim, hps):
    if has_prefix:
        q_ref, k_ref, v_ref, pk_ref, pv_ref, o_ref, m_ref, acc_ref = refs
    else:
        q_ref, k_ref, v_ref, o_ref, m_ref, acc_ref = refs
    i = pl.program_id(2)
    cols = lambda hd: slice(hd * HEAD_SLOT, (hd + 1) * HEAD_SLOT)
    if has_prefix:
        for hd in range(hps):
            _attn_update(hd, 0, tq, q_ref, pk_ref[:, cols(hd)], pv_ref[:, cols(hd)], None, m_ref, acc_ref, True)
        nfull = i * (tq // tk)

        def body(j, carry):
            k0 = pl.multiple_of(j * tk, tk)
            for hd in range(hps):
                _attn_update(hd, 0, tq, q_ref, k_ref[pl.ds(k0, tk), cols(hd)], v_ref[pl.ds(k0, tk), cols(hd)],
                             None, m_ref, acc_ref, False)
            return carry

        lax.fori_loop(0, nfull, body, 0)
    for jj in range(tq // dk):
        r0 = jj * dk
        nr = tq - r0
        k0 = pl.multiple_of(i * tq + r0, dk)
        row = lax.broadcasted_iota(jnp.int32, (nr, dk), 0)
        col = lax.broadcasted_iota(jnp.int32, (nr, dk), 1)
        mask = col <= row
        for hd in range(hps):
            _attn_update(hd, r0, nr, q_ref, k_ref[pl.ds(k0, dk), cols(hd)], v_ref[pl.ds(k0, dk), cols(hd)], mask,
                         m_ref, acc_ref, (not has_prefix) and jj == 0)
    lane = lax.broadcasted_iota(jnp.int32, (tq, LANES), 1)
    for pr in range(hps // 2):
        a0 = acc_ref[2 * pr]
        a1 = acc_ref[2 * pr + 1]
        o0 = a0 / pltpu.roll(a0, vdim, axis=1)
        o1 = pltpu.roll(a1, vdim, axis=1) / a1
        o_ref[:, pr * LANES:(pr + 1) * LANES] = jnp.where(lane < vdim, o0, o1).astype(o_ref.dtype)


def _attn_call(q, k, v, pk, pv, tq, tk, dk, nheads, vdim):
    bsz, t, _ = q.shape
    hps = ATTN_HEADS_PER_STEP if nheads % ATTN_HEADS_PER_STEP == 0 else 2
    assert 2 * vdim == HEAD_SLOT and nheads % hps == 0 and t % tq == 0 and tq % tk == 0 and tq % dk == 0
    has_prefix = pk is not None
    assert has_prefix or t == tq
    tile = pl.BlockSpec((None, tq, hps * HEAD_SLOT), lambda bi, hg, i: (bi, i, hg))
    seq = pl.BlockSpec((None, t, hps * HEAD_SLOT), lambda bi, hg, i: (bi, 0, hg))
    in_specs = [tile, seq, seq]
    args = [q, k, v]
    if has_prefix:
        pre = pl.BlockSpec((pk.shape[0], hps * HEAD_SLOT), lambda bi, hg, i: (0, hg))
        in_specs += [pre, pre]
        args += [pk, pv]
    return pl.pallas_call(
        functools.partial(_attn_kernel, tq=tq, tk=tk, dk=dk, has_prefix=has_prefix, vdim=vdim, hps=hps),
        grid=(bsz, nheads // hps, t // tq),
        in_specs=in_specs,
        out_specs=pl.BlockSpec((None, tq, hps * vdim), lambda bi, hg, i: (bi, i, hg)),
        out_shape=jax.ShapeDtypeStruct((bsz, t, nheads * vdim), BF16),
        scratch_shapes=[pltpu.VMEM((hps, tq, LANES), F32)] * 2,
        compiler_params=_cparams(("parallel", "parallel", "arbitrary")),
        name="mla_prompt_attn",
    )(*args)


def _qt_kernel(q_ref, w_ref, o_ref, *, nheads):
    for hd in range(nheads):
        o_ref[hd] = _dot(q_ref[:, hd * HEAD_SLOT:(hd + 1) * HEAD_SLOT], w_ref[hd]).astype(BF16)


def _qt_call(q, w, nheads):
    r = q.shape[0]
    return pl.pallas_call(
        functools.partial(_qt_kernel, nheads=nheads),
        out_shape=jax.ShapeDtypeStruct((nheads, r, LANES), BF16),
        compiler_params=pltpu.CompilerParams(vmem_limit_bytes=VMEM_LIMIT),
        name="mla_absorb_q",
    )(q, w)


def _uv_kernel(ol_ref, w_ref, o_ref, *, nheads):
    for hp in range(nheads // 2):
        r = (_dot(ol_ref[2 * hp].astype(BF16), w_ref[2 * hp])
             + _dot(ol_ref[2 * hp + 1].astype(BF16), w_ref[2 * hp + 1]))
        o_ref[:, hp * LANES:(hp + 1) * LANES] = r.astype(BF16)


def _uv_call(ol, w, nheads, vdim):
    r = ol.shape[1]
    return pl.pallas_call(
        functools.partial(_uv_kernel, nheads=nheads),
        out_shape=jax.ShapeDtypeStruct((r, nheads * vdim), BF16),
        compiler_params=pltpu.CompilerParams(vmem_limit_bytes=VMEM_LIMIT),
        name="mla_value_up",
    )(ol, w)


def _paged_scores(lhs_ref, qr_ref, cb, krt, nheads, nq, nope):
    n = cb.shape[0]
    r = _dot_nt(lhs_ref[...], cb)
    kt = r[:nheads * nope]
    sraw = r[nheads * nope:]
    ss = jnp.sum((kt * kt).reshape(nheads, nope, n), axis=1)
    rs = lax.rsqrt(ss * (1.0 / nope) + NORM_EPS)
    srope = _dot(qr_ref[...], krt)
    return (sraw.reshape(nq, nheads, n) * rs[None, :, :]).reshape(nq * nheads, n) + srope


def _paged_kernel(pt_ref, lat_hbm, krt_hbm, wukt_ref, qt_ref, qr_ref, cn_ref, krn_ref, o_ref,
                  latbuf, krbuf, sem, lhs_ref, *, npages, ch, sub, nheads, nq, nope):
    b = pl.program_id(0)
    nb = pl.num_programs(0)
    page = latbuf.shape[1] // ch

    def lat_copy(pid, slot, p):
        return pltpu.make_async_copy(lat_hbm.at[pid], latbuf.at[slot, pl.ds(p * page, page), :], sem.at[slot, 0])

    def krt_copy(pid, slot, p):
        return pltpu.make_async_copy(krt_hbm.at[pid], krbuf.at[slot, :, pl.ds(p * page, page)], sem.at[slot, 1])

    def start_chunk(bb, c, slot):
        for p in range(ch):
            pid = pt_ref[bb * npages + c * ch + p]
            lat_copy(pid, slot, p).start()
            krt_copy(pid, slot, p).start()

    def wait_chunk(slot):
        for p in range(ch):
            lat_copy(0, slot, p).wait()
            krt_copy(0, slot, p).wait()

    def chunk_scores(slot):
        cbs, scores = [], []
        for sb in range(ch * page // sub):
            keys = slice(sb * sub, (sb + 1) * sub)
            cbs.append(latbuf[slot, keys, :].astype(BF16))
            krt = krbuf[slot, :, keys].astype(BF16)
            scores.append(_paged_scores(lhs_ref, qr_ref, cbs[-1], krt, nheads, nq, nope))
        return cbs, scores

    def softmax_block(scores, cbs):
        s = jnp.concatenate(scores, axis=1)
        m = jnp.max(s, axis=1, keepdims=True)
        p = jnp.exp2(s - m)
        return m, jnp.sum(p, axis=1, keepdims=True), _dot(p.astype(BF16), jnp.concatenate(cbs, axis=0))

    @pl.when(b == 0)
    def _():
        start_chunk(0, 0, 0)
        start_chunk(0, 1, 1)
        lhs_ref[0:nheads * nope, :] = wukt_ref[...]

    lhs_ref[nheads * nope:, :] = qt_ref[...]
    nxt = jnp.where(b + 1 < nb, b + 1, 0)

    wait_chunk(0)
    cbs, scores = chunk_scores(0)
    cnb = cn_ref[...].astype(BF16)
    snew = _paged_scores(lhs_ref, qr_ref, cnb, krn_ref[...].astype(BF16), nheads, nq, nope)
    row = lax.broadcasted_iota(jnp.int32, snew.shape, 0)
    col = lax.broadcasted_iota(jnp.int32, snew.shape, 1)
    snew = jnp.where(col * nheads <= row, snew, NEG)
    start_chunk(nxt, 0, 0)
    m0, l0, a0 = softmax_block([snew] + scores, [cnb] + cbs)

    wait_chunk(1)
    cbs, scores = chunk_scores(1)
    start_chunk(nxt, 1, 1)
    m1, l1, a1 = softmax_block(scores, cbs)

    m = jnp.maximum(m0, m1)
    w0 = jnp.exp2(m0 - m)
    w1 = jnp.exp2(m1 - m)
    o_ref[...] = (w0 * a0 + w1 * a1) / (w0 * l0 + w1 * l1)

    @pl.when(b == nb - 1)
    def _():
        wait_chunk(0)
        wait_chunk(1)


def _paged_call(page_table, cache_lat, cache_krt, wukt, qt, qr, cn, krn, nheads, nq, nope):
    db, npages = page_table.shape
    _, page, clat = cache_lat.shape
    rdim = cache_krt.shape[1]
    assert npages % 2 == 0
    ch = npages // 2
    sub = min(PAGED_SUB_KEYS, ch * page)
    assert (ch * page) % sub == 0
    nrow = nq * nheads
    per_b = lambda shape: pl.BlockSpec((None,) + shape, lambda b, pt: (b, 0, 0))
    in_specs = [pl.BlockSpec(memory_space=pl.ANY), pl.BlockSpec(memory_space=pl.ANY),
                pl.BlockSpec(wukt.shape, lambda b, pt: (0, 0)),
                per_b((nrow, clat)), per_b((nrow, rdim)), per_b(cn.shape[1:]), per_b(krn.shape[1:])]
    return pl.pallas_call(
        functools.partial(_paged_kernel, npages=npages, ch=ch, sub=sub, nheads=nheads, nq=nq, nope=nope),
        grid_spec=pltpu.PrefetchScalarGridSpec(
            num_scalar_prefetch=1,
            grid=(db,),
            in_specs=in_specs,
            out_specs=per_b((nrow, clat)),
            scratch_shapes=[pltpu.VMEM((2, ch * page, clat), F32), pltpu.VMEM((2, rdim, ch * page), F32),
                            pltpu.SemaphoreType.DMA((2, 2)), pltpu.VMEM((wukt.shape[0] + nrow, clat), BF16)]),
        out_shape=jax.ShapeDtypeStruct((db, nrow, clat), F32),
        compiler_params=_cparams(("arbitrary",)),
        name="mla_paged_attn",
    )(page_table.reshape(-1), cache_lat, cache_krt, wukt, qt, qr, cn, krn)


def _slot(x, lo, width=HEAD_SLOT):
    n = x.shape[-1]
    pad = [(0, 0)] * (x.ndim - 1) + [(lo, width - lo - n)]
    return jnp.pad(x, pad)


def _half_rot_cols(w):
    half = w.shape[-1] // 2
    return jnp.concatenate([-w[..., half:], w[..., :half]], axis=-1)


def _rope_tables(pos, rope, nope):
    half = rope // 2
    inv = ROPE_THETA ** (-jnp.arange(half, dtype=F32) / half)
    ang = pos.astype(F32)[:, None] * inv[None, :]
    cos, sin = jnp.cos(ang), jnp.sin(ang)
    cs = jnp.concatenate([jnp.ones((pos.shape[0], nope), F32), cos, cos], axis=1)
    sn = jnp.concatenate([jnp.zeros((pos.shape[0], nope), F32), sin, sin], axis=1)
    return _slot(cs, 0), _slot(sn, 0)


def kernel(x_prompt, x_sample, state_conv_a, state_ffn_conv, cache_kv_latent, cache_k_rope, page_table, meta_tokens, norm_mix, norm_ffn, a_w_pw1, a_b_pw1, a_w_dw, a_b_dw, a_ln_g, a_ln_b, a_w_pw2, a_b_pw2, ffn_w_up, ffn_w_dw, ffn_w_down, kv_norm, mla_w_dkv, mla_lat_norm, mla_w_kr, mla_knorm_rope, mla_w_uk, mla_w_uv, mla_knorm_nope, mla_w_dq, mla_q_lat_norm, mla_w_uq, mla_qnorm_nope, mla_qnorm_rope, mla_w_o):
    bsz, seq, d = x_prompt.shape
    db, nq, _ = x_sample.shape
    nmeta = meta_tokens.shape[0]
    depth = norm_mix.shape[0]
    n_a = a_w_pw1.shape[0]
    n_b = mla_w_dq.shape[0]
    f = ffn_w_down.shape[1]
    clat, nheads, nope = mla_w_uk.shape
    rope = mla_w_kr.shape[1]
    vdim = mla_w_uv.shape[2]
    width_a = a_w_dw.shape[1]
    npages, page = page_table.shape[1], cache_kv_latent.shape[1]
    past_len = npages * page
    assert n_b == 1 and depth == n_a + n_b and clat == LANES and nope + rope <= HEAD_SLOT
    assert seq >= width_a - 1 and nmeta % 16 == 0 and nmeta >= 8
    tm = ROW_TILE if seq % ROW_TILE == 0 else seq
    rs = nq * db
    row2 = lambda v: v.reshape(1, -1)

    hp = x_prompt
    hm = meta_tokens.astype(F32)[None]
    hs = jnp.transpose(x_sample, (1, 0, 2)).reshape(1, rs, d)

    conv_a_p, conv_a_s, ffn_p, ffn_s = [], [], [], []

    def run_ffn(layer, hp, hm, hs):
        g = row2(norm_ffn[layer])
        wup = ffn_w_up[layer].astype(BF16)
        wdw = jnp.transpose(ffn_w_dw[layer].reshape(3, 2, f), (1, 0, 2))
        wdn = ffn_w_down[layer].astype(BF16)
        st = jnp.transpose(state_ffn_conv[layer].reshape(db, 2, 2, f), (2, 1, 0, 3))
        hm_new, _ = _ffn_call(hm, g, wup, wdw, wdn, nmeta)
        hp_new, tail_p = _ffn_call(hp, g, wup, wdw, wdn, tm, prev=hm[0, nmeta - 8:])
        hs_new, tail_s = _ffn_call(hs, g, wup, wdw, wdn, rs, state=st)
        ffn_p.append(jnp.transpose(tail_p[:, :, 6:8, :], (0, 2, 1, 3)).reshape(bsz, 2, 2 * f))
        ffn_s.append(jnp.transpose(tail_s[0].reshape(2, 2, db, f), (2, 1, 0, 3)).reshape(db, 2, 2 * f))
        return hp_new, hm_new, hs_new

    for layer in range(n_a):
        g = row2(norm_mix[layer])
        w1 = a_w_pw1[layer].astype(BF16)
        b1 = row2(a_b_pw1[layer])
        conv_w = (a_w_dw[layer], row2(a_b_dw[layer]), row2(a_ln_g[layer]), row2(a_ln_b[layer]),
                  a_w_pw2[layer].astype(BF16), row2(a_b_pw2[layer]))
        um = _a1_call(hm, g, w1, b1, nmeta)
        up = _a1_call(hp, g, w1, b1, tm)
        us = _a1_call(hs, g, w1, b1, rs)
        halo = jnp.pad(um[0], ((32 - nmeta, 0), (0, 0))) if nmeta < 32 else um[0, nmeta - 32:]
        hm_new = _a2_seq_call(um, hm, None, *conv_w, nmeta)
        hp = _a2_seq_call(up, hp, halo, *conv_w, tm)
        hm = hm_new
        xp_s = jnp.concatenate([jnp.transpose(state_conv_a[layer], (1, 0, 2)), us.reshape(nq, db, d)], axis=0)
        hs = _a2_tm_call(xp_s, hs.reshape(nq, db, d), *conv_w, 32 if db % 32 == 0 else db).reshape(1, rs, d)
        conv_a_p.append(up[:, seq - (width_a - 1):])
        conv_a_s.append(jnp.transpose(xp_s[nq:], (1, 0, 2)))
        hp, hm, hs = run_ffn(layer, hp, hm, hs)

    layer = n_a
    g_kr = mla_knorm_rope
    wkr_g = mla_w_kr * g_kr[None, :]
    uq = mla_w_uq[0]
    uq_rope_g = uq[:, :, nope:] * mla_qnorm_rope[0][None, None, :]
    pw = {
        "gkv": row2(kv_norm), "gmix": row2(norm_mix[layer]),
        "wkv": jnp.concatenate([mla_w_dkv, _slot(mla_w_kr, nope), _slot(_half_rot_cols(wkr_g), nope)], axis=1).astype(BF16),
        "glat": row2(mla_lat_norm), "gkr": _slot(row2(g_kr), nope),
        "wdq": mla_w_dq[0].astype(BF16), "gql": row2(mla_q_lat_norm[0]),
        "wqm": _slot(uq, 0).reshape(uq.shape[0], nheads * HEAD_SLOT).astype(BF16),
        "wqr": _slot(_half_rot_cols(uq_rope_g), nope).reshape(uq.shape[0], nheads * HEAD_SLOT).astype(BF16),
        "gq": _slot(row2(jnp.concatenate([mla_qnorm_nope[0], mla_qnorm_rope[0]])), 0),
        "wuk": _slot(mla_w_uk, 0).reshape(clat, nheads * HEAD_SLOT).astype(BF16),
        "gk": _slot(row2(mla_knorm_nope), 0),
        "wuv": _slot(mla_w_uv, 0).reshape(clat, nheads * HEAD_SLOT).astype(BF16),
        "vone": jnp.tile(_slot(jnp.ones((1, HEAD_SLOT - vdim), F32), vdim), (1, nheads)),
    }
    cs_m, sn_m = _rope_tables(jnp.arange(nmeta), rope, nope)
    cs_p, sn_p = _rope_tables(nmeta + jnp.arange(seq), rope, nope)
    cs_s, sn_s = _rope_tables(jnp.repeat(past_len + jnp.arange(nq), db), rope, nope)
    proj = functools.partial(_proj_call, w=pw, nheads=nheads, nope=nope, rope=rope)
    c_m, kr_m, q_m, k_m, v_m = proj(hm, cs_m, sn_m, tm=nmeta)
    c_p, kr_p, q_p, k_p, v_p = proj(hp, cs_p, sn_p, tm=tm)
    c_s, kr_s, q_s, _, _ = proj(hs, cs_s, sn_s, tm=rs)
    kr_m, kr_p, kr_s = (x[..., nope:nope + rope] for x in (kr_m, kr_p, kr_s))

    tq = ATTN_Q_TILE if seq % ATTN_Q_TILE == 0 else seq
    tk = ATTN_K_TILE if tq % ATTN_K_TILE == 0 else tq
    dk = ATTN_DIAG_TILE if tq % ATTN_DIAG_TILE == 0 else tq
    o_m = _attn_call(q_m, k_m, v_m, None, None, nmeta, nmeta, nmeta, nheads, vdim)
    o_p = _attn_call(q_p, k_p, v_p, k_m[0], v_m[0], tq, tk, dk, nheads, vdim)

    wukt = jnp.transpose(mla_w_uk.reshape(clat, nheads * nope)).astype(BF16)
    wqt = jnp.transpose(mla_w_uk, (1, 2, 0)) * mla_knorm_nope[None, :, None]
    wqt = jnp.pad(wqt, ((0, 0), (0, HEAD_SLOT - nope), (0, 0))).astype(BF16)
    qt = _qt_call(q_s[0], wqt, nheads)
    qt = jnp.transpose(qt.reshape(nheads, nq, db, clat), (2, 1, 0, 3)).reshape(db, nq * nheads, clat)
    qr = jnp.transpose(q_s[0].reshape(nq, db, nheads, HEAD_SLOT)[..., nope:nope + rope], (1, 0, 2, 3))
    qr = qr.reshape(db, nq * nheads, rope)
    nnew = LANES
    cn = jnp.pad(jnp.transpose(c_s[0].reshape(nq, db, clat), (1, 0, 2)), ((0, 0), (0, nnew - nq), (0, 0)))
    krn = jnp.pad(jnp.transpose(kr_s[0].reshape(nq, db, rope), (1, 2, 0)), ((0, 0), (0, 0), (0, nnew - nq)))
    cache_krt = jnp.transpose(cache_k_rope, (0, 2, 1))
    o_lat = _paged_call(page_table, cache_kv_latent, cache_krt, wukt, qt, qr, cn, krn, nheads, nq, nope)
    ol = jnp.transpose(o_lat.reshape(db, nq, nheads, clat), (2, 1, 0, 3)).reshape(nheads, rs, clat)
    wuv_pad = jnp.stack([_slot(mla_w_uv[:, hd, :], (hd % 2) * vdim, LANES) for hd in range(nheads)]).astype(BF16)
    o_s = _uv_call(ol, wuv_pad, nheads, vdim)[None]

    wo = mla_w_o[0].astype(BF16)
    hm = _wo_call(hm, o_m, wo, nmeta)
    hp = _wo_call(hp, o_p, wo, tm)
    hs = _wo_call(hs, o_s, wo, rs)
    hp, hm, hs = run_ffn(layer, hp, hm, hs)

    y_sample = jnp.transpose(hs.reshape(nq, db, d), (1, 0, 2))
    kv_lat_p = jnp.concatenate([jnp.broadcast_to(c_m, (bsz, nmeta, clat)), c_p], axis=1)
    k_rope_p = jnp.concatenate([jnp.broadcast_to(kr_m, (bsz, nmeta, rope)), kr_p], axis=1)
    kv_lat_s = jnp.transpose(c_s[0].reshape(nq, db, clat), (1, 0, 2))
    k_rope_s = jnp.transpose(kr_s[0].reshape(nq, db, rope), (1, 0, 2))
    return (hp, y_sample, jnp.stack(conv_a_p), jnp.stack(conv_a_s), jnp.stack(ffn_p), jnp.stack(ffn_s),
            kv_lat_p, k_rope_p, kv_lat_s, k_rope_s)
```

```python
import functools
import math

import jax
import jax.numpy as jnp
from jax import lax
from jax.experimental import pallas as pl
from jax.experimental.pallas import tpu as pltpu

F32 = jnp.float32
BF16 = jnp.bfloat16
NORM_EPS = 1e-6
ROPE_THETA = 10000.0
NEG = -1e30

LANES = 128
HEAD_SLOT = 128
ROW_TILE = 512
ATTN_Q_TILE = 1024
ATTN_K_TILE = 1024
ATTN_DIAG_TILE = 512
FFN_CHUNK = 256
FFN_SEQ_CHUNK = 2816
ATTN_HEADS_PER_STEP = 4
PAGED_SUB_KEYS = 2048
LOG2E = 1.4426950408889634
VMEM_LIMIT = 56 * 1024 * 1024


def _cparams(sem):
    return pltpu.CompilerParams(dimension_semantics=sem, vmem_limit_bytes=VMEM_LIMIT)


def _const(shape):
    zeros = (0,) * len(shape)
    return pl.BlockSpec(shape, lambda *_: zeros)


def _sigmoid(x):
    return 1.0 / (1.0 + jnp.exp(-x))


def _rms_scale(x):
    return lax.rsqrt(jnp.mean(x * x, axis=-1, keepdims=True) + NORM_EPS)


def _dot(a, b):
    return jnp.dot(a, b, preferred_element_type=F32)


def _dot_nt(a, b):
    return lax.dot_general(a, b, (((1,), (1,)), ((), ())), preferred_element_type=F32)


def _a1_kernel(x_ref, g_ref, w_ref, b_ref, u_ref):
    x = x_ref[...]
    xn = x * _rms_scale(x) * g_ref[...]
    z = _dot(xn.astype(BF16), w_ref[...]) + b_ref[...]
    d = u_ref.shape[-1]
    u_ref[...] = z[:, :d] * _sigmoid(z[:, d:])


def _a1_call(x, g, w, b, tm):
    bsz, t, d = x.shape
    return pl.pallas_call(
        _a1_kernel,
        grid=(bsz, t // tm),
        in_specs=[pl.BlockSpec((None, tm, d), lambda bi, i: (bi, i, 0)),
                  _const((1, d)), _const((d, 2 * d)), _const((1, 2 * d))],
        out_specs=pl.BlockSpec((None, tm, d), lambda bi, i: (bi, i, 0)),
        out_shape=jax.ShapeDtypeStruct((bsz, t, d), F32),
        compiler_params=_cparams(("parallel", "parallel")),
        name="conv_pw1_glu",
    )(x, g, w, b)


def _ln_swish_pw2(y, h, lng_ref, lnb_ref, w2_ref, b2_ref):
    mu = jnp.mean(y, axis=-1, keepdims=True)
    yc = y - mu
    var = jnp.mean(yc * yc, axis=-1, keepdims=True)
    yn = yc * lax.rsqrt(var + NORM_EPS) * lng_ref[...] + lnb_ref[...]
    a = yn * _sigmoid(yn)
    return h + _dot(a.astype(BF16), w2_ref[...]) + b2_ref[...]


def _a2_seq_kernel(*refs, tm, halo, width, has_prev):
    if has_prev:
        (u_ref, uh_ref, prev_ref, h_ref, wdw_ref, bdw_ref, lng_ref, lnb_ref, w2_ref, b2_ref,
         o_ref, ext_ref, y_ref) = refs
        i = pl.program_id(1)

        @pl.when(i == 0)
        def _():
            ext_ref[0:halo, :] = prev_ref[...]

        @pl.when(i > 0)
        def _():
            ext_ref[0:halo, :] = uh_ref[...]
    else:
        (u_ref, h_ref, wdw_ref, bdw_ref, lng_ref, lnb_ref, w2_ref, b2_ref,
         o_ref, ext_ref, y_ref) = refs
        ext_ref[0:halo, :] = jnp.zeros((halo, ext_ref.shape[1]), F32)
    ext_ref[halo:halo + tm, :] = u_ref[...]
    d = u_ref.shape[-1]
    off = halo - (width - 1)

    taps = {}
    for k in range(width):
        taps.setdefault((off + k) % 8, []).append(((off + k) // 8, k))
    nblk = tm // 8

    def col_body(c, carry):
        c0 = pl.multiple_of(c * LANES, LANES)
        cols = pl.ds(c0, LANES)
        bias = jnp.broadcast_to(bdw_ref[:, cols], (8, LANES))
        sub = lax.broadcasted_iota(jnp.int32, (8, LANES), 0)
        xs = {}

        def xblk(jb):
            if jb not in xs:
                xs[jb] = ext_ref[pl.ds(jb * 8, 8), cols]
            return xs[jb]

        def zsum(r, jb):
            acc = None
            for a, k in taps[r]:
                term = wdw_ref[pl.ds(k * 8, 8), cols] * xblk(jb + a)
                acc = term if acc is None else acc + term
            return acc

        zprev = {r: zsum(r, 0) for r in taps if r != 0}
        for jb in range(nblk):
            y = zsum(0, jb) + bias if 0 in taps else bias
            znext = {}
            for r in zprev:
                znext[r] = zsum(r, jb + 1)
                y = y + pltpu.roll(jnp.where(sub >= r, zprev[r], znext[r]), 8 - r, axis=0)
            y_ref[pl.ds(jb * 8, 8), cols] = y
            zprev = znext
        return carry

    lax.fori_loop(0, d // LANES, col_body, 0)
    o_ref[...] = _ln_swish_pw2(y_ref[...], h_ref[...], lng_ref, lnb_ref, w2_ref, b2_ref)


def _a2_seq_call(u, h, prev, wdw, bdw, lng, lnb, w2, b2, tm):
    bsz, t, d = u.shape
    width = wdw.shape[0]
    halo = 32
    assert halo >= width - 1 and halo % 8 == 0 and (tm % halo == 0 or prev is None)
    wdw = jnp.repeat(wdw, 8, axis=0)
    row = pl.BlockSpec((None, tm, d), lambda bi, i: (bi, i, 0))
    consts = [_const((width * 8, d)), _const((1, d)), _const((1, d)), _const((1, d)), _const((d, d)), _const((1, d))]
    if prev is not None:
        k = tm // halo
        in_specs = [row, pl.BlockSpec((None, halo, d), lambda bi, i: (bi, jnp.maximum(i * k - 1, 0), 0)),
                    _const((halo, d)), row] + consts
        args = (u, u, prev, h, wdw, bdw, lng, lnb, w2, b2)
    else:
        assert t == tm
        in_specs = [row, row] + consts
        args = (u, h, wdw, bdw, lng, lnb, w2, b2)
    return pl.pallas_call(
        functools.partial(_a2_seq_kernel, tm=tm, halo=halo, width=width, has_prev=prev is not None),
        grid=(bsz, t // tm),
        in_specs=in_specs,
        out_specs=row,
        out_shape=jax.ShapeDtypeStruct((bsz, t, d), F32),
        scratch_shapes=[pltpu.VMEM((halo + tm, d), F32), pltpu.VMEM((tm, d), F32)],
        compiler_params=_cparams(("parallel", "parallel")),
        name="conv_dw_ln_pw2",
    )(*args)


def _a2_tm_kernel(xp_ref, h_ref, wdw_ref, bdw_ref, lng_ref, lnb_ref, w2_ref, b2_ref, o_ref, y_ref, *, width, cw):
    nt, bs, d = h_ref.shape
    for c in range(d // cw):
        cols = slice(c * cw, (c + 1) * cw)
        wk = [jnp.broadcast_to(wdw_ref[k:k + 1, cols], (bs, cw)) for k in range(width)]
        bias = bdw_ref[:, cols]
        for t in range(nt):
            acc = wk[0] * xp_ref[t, :, cols]
            for k in range(1, width):
                acc = acc + wk[k] * xp_ref[t + k, :, cols]
            y_ref[t * bs:(t + 1) * bs, cols] = acc + bias
    h = h_ref[...].reshape(nt * bs, d)
    out = _ln_swish_pw2(y_ref[...], h, lng_ref, lnb_ref, w2_ref, b2_ref)
    o_ref[...] = out.reshape(nt, bs, d)


def _a2_tm_call(xp, h, wdw, bdw, lng, lnb, w2, b2, bs):
    nt, db, d = h.shape
    width = wdw.shape[0]
    return pl.pallas_call(
        functools.partial(_a2_tm_kernel, width=width, cw=2 * LANES),
        grid=(db // bs,),
        in_specs=[pl.BlockSpec((nt + width - 1, bs, d), lambda i: (0, i, 0)),
                  pl.BlockSpec((nt, bs, d), lambda i: (0, i, 0)),
                  _const((width, d)), _const((1, d)), _const((1, d)), _const((1, d)), _const((d, d)), _const((1, d))],
        out_specs=pl.BlockSpec((nt, bs, d), lambda i: (0, i, 0)),
        out_shape=jax.ShapeDtypeStruct((nt, db, d), F32),
        scratch_shapes=[pltpu.VMEM((nt * bs, d), F32)],
        compiler_params=_cparams(("parallel",)),
        name="conv_dw_ln_pw2_sample",
    )(xp, h, wdw, bdw, lng, lnb, w2, b2)


def _ffn_kernel(*refs, mode, tm, fc, shift, ngroups):
    if mode == "prev":
        x_ref, xh_ref, prev_ref, g_ref, wg_ref, wu_ref, wdw_ref, wdn_ref, o_ref, tail_ref = refs
    elif mode == "zero":
        x_ref, g_ref, wg_ref, wu_ref, wdw_ref, wdn_ref, o_ref, tail_ref = refs
    else:
        x_ref, st_ref, g_ref, wg_ref, wu_ref, wdw_ref, wdn_ref, o_ref, tail_ref = refs
    x = x_ref[...]
    d = x.shape[-1]
    f = wdn_ref.shape[0]
    if mode == "prev":
        first = pl.program_id(1) == 0
        halo = jnp.where(first, prev_ref[...], xh_ref[...])
        xe = jnp.concatenate([halo, x], axis=0)
    elif mode == "zero":
        xe = jnp.concatenate([jnp.zeros((8, d), F32), x], axis=0)
    else:
        xe = x
    pre = xe.shape[0] - tm if mode != "state" else 2 * shift
    xn = (xe * _rms_scale(xe) * g_ref[...]).astype(BF16)
    ntail = tail_ref.shape[1]
    fg = wdn_ref.shape[0]
    acc = jnp.zeros((tm, d), F32)
    for j in range(fg // fc):
        cols = slice(j * fc, (j + 1) * fc)
        act = None
        for gu in range(2):
            hu = _dot(xn, (wg_ref, wu_ref)[gu][:, cols])
            tail_ref[gu, :, cols] = hu[hu.shape[0] - ntail:, :]
            if mode == "state":
                hu = jnp.concatenate([st_ref[gu, 0, :, cols], st_ref[gu, 1, :, cols], hu], axis=0)
            w = wdw_ref[gu, :, cols]
            conv = (w[0:1] * hu[pre - 2 * shift:pre - 2 * shift + tm]
                    + w[1:2] * hu[pre - shift:pre - shift + tm]
                    + w[2:3] * hu[pre:pre + tm])
            act = conv * _sigmoid(conv) if gu == 0 else act * conv
        acc = acc + _dot(act.astype(BF16), wdn_ref[cols, :])
    if ngroups == 1:
        o_ref[...] = x + acc
    else:
        gj = pl.program_id(2)

        @pl.when(gj == 0)
        def _():
            o_ref[...] = x + acc

        @pl.when(gj > 0)
        def _():
            o_ref[...] = o_ref[...] + acc


def _ffn_call(x, g, wup, wdw, wdn, tm, prev=None, state=None):
    bsz, t, d = x.shape
    f = wdn.shape[0]
    fc = FFN_CHUNK if f % FFN_CHUNK == 0 else LANES
    if state is None and f % FFN_SEQ_CHUNK == 0:
        fc = FFN_SEQ_CHUNK
    fg = fc if state is not None else f
    ngroups = f // fg
    row = pl.BlockSpec((None, tm, d), lambda bi, i, gj: (bi, i, 0))
    consts = [pl.BlockSpec((1, d), lambda bi, i, gj: (0, 0)),
              pl.BlockSpec((d, fg), lambda bi, i, gj: (0, gj)),
              pl.BlockSpec((d, fg), lambda bi, i, gj: (0, ngroups + gj)),
              pl.BlockSpec((2, 3, fg), lambda bi, i, gj: (0, 0, gj)),
              pl.BlockSpec((fg, d), lambda bi, i, gj: (gj, 0))]
    if state is not None:
        assert bsz == 1 and t == tm
        mode, shift, ntail = "state", state.shape[2], 2 * state.shape[2]
        in_specs = [row, pl.BlockSpec(state.shape[:3] + (fg,), lambda bi, i, gj: (0, 0, 0, gj))] + consts
        args = (x, state, g, wup, wup, wdw, wdn)
    elif prev is not None:
        mode, shift, ntail = "prev", 1, 8
        k = tm // 8
        in_specs = [row, pl.BlockSpec((None, 8, d), lambda bi, i, gj: (bi, jnp.maximum(i * k - 1, 0), 0)),
                    pl.BlockSpec((8, d), lambda bi, i, gj: (0, 0))] + consts
        args = (x, x, prev, g, wup, wup, wdw, wdn)
    else:
        assert t == tm
        mode, shift, ntail = "zero", 1, 8
        in_specs = [row] + consts
        args = (x, g, wup, wup, wdw, wdn)
    out, tail = pl.pallas_call(
        functools.partial(_ffn_kernel, mode=mode, tm=tm, fc=fc, shift=shift, ngroups=ngroups),
        grid=(bsz, t // tm, ngroups),
        in_specs=in_specs,
        out_specs=[row, pl.BlockSpec((None, 2, ntail, fg), lambda bi, i, gj: (bi, 0, 0, gj))],
        out_shape=[jax.ShapeDtypeStruct((bsz, t, d), F32), jax.ShapeDtypeStruct((bsz, 2, ntail, f), F32)],
        compiler_params=_cparams(("parallel", "arbitrary", "arbitrary")),
        name="conv_ffn_" + mode,
    )(*args)
    return out, tail


def _wo_kernel(h_ref, o_ref, w_ref, out_ref):
    out_ref[...] = h_ref[...] + _dot(o_ref[...], w_ref[...])


def _wo_call(h, o, w, tm):
    bsz, t, d = h.shape
    hv = o.shape[-1]
    return pl.pallas_call(
        _wo_kernel,
        grid=(bsz, t // tm),
        in_specs=[pl.BlockSpec((None, tm, d), lambda bi, i: (bi, i, 0)),
                  pl.BlockSpec((None, tm, hv), lambda bi, i: (bi, i, 0)), _const((hv, d))],
        out_specs=pl.BlockSpec((None, tm, d), lambda bi, i: (bi, i, 0)),
        out_shape=jax.ShapeDtypeStruct((bsz, t, d), F32),
        compiler_params=_cparams(("parallel", "parallel")),
        name="attn_out_proj",
    )(h, o, w)


def _proj_kernel(h_ref, cs_ref, sn_ref, gkv_ref, gmix_ref, wkv_ref, glat_ref, gkr_ref, wdq_ref, gql_ref,
                 wqm_ref, wqr_ref, gq_ref, wuk_ref, gk_ref, wuv_ref, vone_ref,
                 c_ref, kr_ref, q_ref, k_ref, v_ref, *, nheads, nope, rope, scale):
    h = h_ref[...]
    hs = h * _rms_scale(h)
    hn = (hs * gkv_ref[...]).astype(BF16)
    xq = (hs * gmix_ref[...]).astype(BF16)
    z = _dot(hn, wkv_ref[...])
    cpre, a, ar = z[:, :LANES], z[:, LANES:2 * LANES], z[:, 2 * LANES:3 * LANES]
    c = cpre * _rms_scale(cpre) * glat_ref[...]
    c_ref[...] = c
    cs = cs_ref[...]
    sn = sn_ref[...]
    ra = lax.rsqrt(jnp.sum(a * a, axis=-1, keepdims=True) * (1.0 / rope) + NORM_EPS)
    kr = ra * (a * gkr_ref[...] * cs + ar * sn)
    kr_ref[...] = kr
    cb = c.astype(BF16)
    v_ref[...] = (_dot(cb, wuv_ref[...]) + vone_ref[...]).astype(BF16)
    kn = _dot(cb, wuk_ref[...])
    ql = _dot(xq, wdq_ref[...])
    ql = (ql * _rms_scale(ql) * gql_ref[...]).astype(BF16)
    qm = _dot(ql, wqm_ref[...])
    qr = _dot(ql, wqr_ref[...])
    lane = lax.broadcasted_iota(jnp.int32, (1, HEAD_SLOT), 1)
    mn = (lane < nope).astype(F32)
    mr = jnp.logical_and(lane >= nope, lane < nope + rope).astype(F32)
    csq = cs * (gq_ref[...] * scale)
    snq = sn * scale
    gk = gk_ref[...]
    for hd in range(nheads):
        sl = slice(hd * HEAD_SLOT, (hd + 1) * HEAD_SLOT)
        qh = qm[:, sl]
        sq = qh * qh
        rn = lax.rsqrt(jnp.sum(sq * mn, axis=-1, keepdims=True) * (1.0 / nope) + NORM_EPS)
        rr = lax.rsqrt(jnp.sum(sq * mr, axis=-1, keepdims=True) * (1.0 / rope) + NORM_EPS)
        qv = (qh * csq + qr[:, sl] * snq) * jnp.where(lane < nope, rn, rr)
        q_ref[:, sl] = qv.astype(BF16)
        kh = kn[:, sl]
        rk = lax.rsqrt(jnp.sum(kh * kh, axis=-1, keepdims=True) * (1.0 / nope) + NORM_EPS)
        k_ref[:, sl] = (kh * rk * gk + kr).astype(BF16)


def _proj_call(h, cs, sn, w, tm, nheads, nope, rope):
    bsz, t, d = h.shape
    ql = w["wdq"].shape[1]
    hs = nheads * HEAD_SLOT
    row = lambda n: pl.BlockSpec((None, tm, n), lambda bi, i: (bi, i, 0))
    tab = pl.BlockSpec((tm, HEAD_SLOT), lambda bi, i: (i, 0))
    in_specs = [row(d), tab, tab, _const((1, d)), _const((1, d)), _const((d, 3 * LANES)), _const((1, LANES)),
                _const((1, HEAD_SLOT)), _const((d, ql)), _const((1, ql)), _const((ql, hs)), _const((ql, hs)),
                _const((1, HEAD_SLOT)), _const((LANES, hs)), _const((1, HEAD_SLOT)), _const((LANES, hs)),
                _const((1, hs))]
    return pl.pallas_call(
        functools.partial(_proj_kernel, nheads=nheads, nope=nope, rope=rope,
                          scale=LOG2E / math.sqrt(nope + rope)),
        grid=(bsz, t // tm),
        in_specs=in_specs,
        out_specs=[row(LANES), row(HEAD_SLOT), row(hs), row(hs), row(hs)],
        out_shape=[jax.ShapeDtypeStruct((bsz, t, LANES), F32), jax.ShapeDtypeStruct((bsz, t, HEAD_SLOT), F32),
                   jax.ShapeDtypeStruct((bsz, t, hs), BF16), jax.ShapeDtypeStruct((bsz, t, hs), BF16),
                   jax.ShapeDtypeStruct((bsz, t, hs), BF16)],
        compiler_params=_cparams(("parallel", "parallel")),
        name="mla_proj",
    )(h, cs, sn, w["gkv"], w["gmix"], w["wkv"], w["glat"], w["gkr"], w["wdq"], w["gql"], w["wqm"], w["wqr"],
      w["gq"], w["wuk"], w["gk"], w["wuv"], w["vone"])


def _attn_update(hd, r0, nr, q_ref, kb, vb, mask, m_ref, acc_ref, first):
    rows = slice(r0, r0 + nr)
    q = q_ref[rows, hd * HEAD_SLOT:(hd + 1) * HEAD_SLOT]
    s = _dot_nt(q, kb)
    if mask is not None:
        s = jnp.where(mask, s, NEG)
    nk = s.shape[1]
    m_cur = jnp.max(s, axis=1, keepdims=True)
    if first:
        m_new = jnp.broadcast_to(m_cur, (nr, LANES))
    else:
        m_prev = m_ref[hd, rows, :]
        m_new = jnp.maximum(m_prev, m_cur)
    if nk % LANES == 0:
        mb = jnp.concatenate([m_new] * (nk // LANES), axis=1) if nk > LANES else m_new
    else:
        mb = m_new[:, :nk]
    p = jnp.exp2((s - mb).astype(BF16))
    pv = _dot(p, vb)
    if first:
        acc_ref[hd, rows, :] = pv
    else:
        alpha = jnp.exp2(m_prev - m_new)
        acc_ref[hd, rows, :] = alpha * acc_ref[hd, rows, :] + pv
    m_ref[hd, rows, :] = m_new


def _attn_kernel(*refs, tq, tk, dk, has_prefix, vdim, hps):
    if has_prefix:
        q_ref, k_ref, v_ref, pk_ref, pv_ref, o_ref, m_ref, acc_ref = refs
    else:
        q_ref, k_ref, v_ref, o_ref, m_ref, acc_ref = refs
    i = pl.program_id(2)
    cols = lambda hd: slice(hd * HEAD_SLOT, (hd + 1) * HEAD_SLOT)
    if has_prefix:
        for hd in range(hps):
            _attn_update(hd, 0, tq, q_ref, pk_ref[:, cols(hd)], pv_ref[:, cols(hd)], None, m_ref, acc_ref, True)
        nfull = i * (tq // tk)

        def body(j, carry):
            k0 = pl.multiple_of(j * tk, tk)
            for hd in range(hps):
                _attn_update(hd, 0, tq, q_ref, k_ref[pl.ds(k0, tk), cols(hd)], v_ref[pl.ds(k0, tk), cols(hd)],
                             None, m_ref, acc_ref, False)
            return carry

        lax.fori_loop(0, nfull, body, 0)
    for jj in range(tq // dk):
        r0 = jj * dk
        nr = tq - r0
        k0 = pl.multiple_of(i * tq + r0, dk)
        row = lax.broadcasted_iota(jnp.int32, (nr, dk), 0)
        col = lax.broadcasted_iota(jnp.int32, (nr, dk), 1)
        mask = col <= row
        for hd in range(hps):
            _attn_update(hd, r0, nr, q_ref, k_ref[pl.ds(k0, dk), cols(hd)], v_ref[pl.ds(k0, dk), cols(hd)], mask,
                         m_ref, acc_ref, (not has_prefix) and jj == 0)
    lane = lax.broadcasted_iota(jnp.int32, (tq, LANES), 1)
    for pr in range(hps // 2):
        a0 = acc_ref[2 * pr]
        a1 = acc_ref[2 * pr + 1]
        o0 = a0 / pltpu.roll(a0, vdim, axis=1)
        o1 = pltpu.roll(a1, vdim, axis=1) / a1
        o_ref[:, pr * LANES:(pr + 1) * LANES] = jnp.where(lane < vdim, o0, o1).astype(o_ref.dtype)


def _attn_call(q, k, v, pk, pv, tq, tk, dk, nheads, vdim):
    bsz, t, _ = q.shape
    hps = ATTN_HEADS_PER_STEP if nheads % ATTN_HEADS_PER_STEP == 0 else 2
    assert 2 * vdim == HEAD_SLOT and nheads % hps == 0 and t % tq == 0 and tq % tk == 0 and tq % dk == 0
    has_prefix = pk is not None
    assert has_prefix or t == tq
    tile = pl.BlockSpec((None, tq, hps * HEAD_SLOT), lambda bi, hg, i: (bi, i, hg))
    seq = pl.BlockSpec((None, t, hps * HEAD_SLOT), lambda bi, hg, i: (bi, 0, hg))
    in_specs = [tile, seq, seq]
    args = [q, k, v]
    if has_prefix:
        pre = pl.BlockSpec((pk.shape[0], hps * HEAD_SLOT), lambda bi, hg, i: (0, hg))
        in_specs += [pre, pre]
        args += [pk, pv]
    return pl.pallas_call(
        functools.partial(_attn_kernel, tq=tq, tk=tk, dk=dk, has_prefix=has_prefix, vdim=vdim, hps=hps),
        grid=(bsz, nheads // hps, t // tq),
        in_specs=in_specs,
        out_specs=pl.BlockSpec((None, tq, hps * vdim), lambda bi, hg, i: (bi, i, hg)),
        out_shape=jax.ShapeDtypeStruct((bsz, t, nheads * vdim), BF16),
        scratch_shapes=[pltpu.VMEM((hps, tq, LANES), F32)] * 2,
        compiler_params=_cparams(("parallel", "parallel", "arbitrary")),
        name="mla_prompt_attn",
    )(*args)


def _qt_kernel(q_ref, w_ref, o_ref, *, nheads):
    for hd in range(nheads):
        o_ref[hd] = _dot(q_ref[:, hd * HEAD_SLOT:(hd + 1) * HEAD_SLOT], w_ref[hd]).astype(BF16)


def _qt_call(q, w, nheads):
    r = q.shape[0]
    return pl.pallas_call(
        functools.partial(_qt_kernel, nheads=nheads),
        out_shape=jax.ShapeDtypeStruct((nheads, r, LANES), BF16),
        compiler_params=pltpu.CompilerParams(vmem_limit_bytes=VMEM_LIMIT),
        name="mla_absorb_q",
    )(q, w)


def _uv_kernel(ol_ref, w_ref, o_ref, *, nheads):
    for hp in range(nheads // 2):
        r = (_dot(ol_ref[2 * hp].astype(BF16), w_ref[2 * hp])
             + _dot(ol_ref[2 * hp + 1].astype(BF16), w_ref[2 * hp + 1]))
        o_ref[:, hp * LANES:(hp + 1) * LANES] = r.astype(BF16)


def _uv_call(ol, w, nheads, vdim):
    r = ol.shape[1]
    return pl.pallas_call(
        functools.partial(_uv_kernel, nheads=nheads),
        out_shape=jax.ShapeDtypeStruct((r, nheads * vdim), BF16),
        compiler_params=pltpu.CompilerParams(vmem_limit_bytes=VMEM_LIMIT),
        name="mla_value_up",
    )(ol, w)


def _paged_scores(lhs_ref, qr_ref, cb, krt, nheads, nq, nope):
    n = cb.shape[0]
    r = _dot_nt(lhs_ref[...], cb)
    kt = r[:nheads * nope]
    sraw = r[nheads * nope:]
    ss = jnp.sum((kt * kt).reshape(nheads, nope, n), axis=1)
    rs = lax.rsqrt(ss * (1.0 / nope) + NORM_EPS)
    srope = _dot(qr_ref[...], krt)
    return (sraw.reshape(nq, nheads, n) * rs[None, :, :]).reshape(nq * nheads, n) + srope


def _paged_kernel(pt_ref, lat_hbm, krt_hbm, wukt_ref, qt_ref, qr_ref, cn_ref, krn_ref, o_ref,
                  latbuf, krbuf, sem, lhs_ref, *, npages, ch, sub, nheads, nq, nope):
    b = pl.program_id(0)
    nb = pl.num_programs(0)
    page = latbuf.shape[1] // ch

    def lat_copy(pid, slot, p):
        return pltpu.make_async_copy(lat_hbm.at[pid], latbuf.at[slot, pl.ds(p * page, page), :], sem.at[slot, 0])

    def krt_copy(pid, slot, p):
        return pltpu.make_async_copy(krt_hbm.at[pid], krbuf.at[slot, :, pl.ds(p * page, page)], sem.at[slot, 1])

    def start_chunk(bb, c, slot):
        for p in range(ch):
            pid = pt_ref[bb * npages + c * ch + p]
            lat_copy(pid, slot, p).start()
            krt_copy(pid, slot, p).start()

    def wait_chunk(slot):
        for p in range(ch):
            lat_copy(0, slot, p).wait()
            krt_copy(0, slot, p).wait()

    def chunk_scores(slot):
        cbs, scores = [], []
        for sb in range(ch * page // sub):
            keys = slice(sb * sub, (sb + 1) * sub)
            cbs.append(latbuf[slot, keys, :].astype(BF16))
            krt = krbuf[slot, :, keys].astype(BF16)
            scores.append(_paged_scores(lhs_ref, qr_ref, cbs[-1], krt, nheads, nq, nope))
        return cbs, scores

    def softmax_block(scores, cbs):
        s = jnp.concatenate(scores, axis=1)
        m = jnp.max(s, axis=1, keepdims=True)
        p = jnp.exp2(s - m)
        return m, jnp.sum(p, axis=1, keepdims=True), _dot(p.astype(BF16), jnp.concatenate(cbs, axis=0))

    @pl.when(b == 0)
    def _():
        start_chunk(0, 0, 0)
        start_chunk(0, 1, 1)
        lhs_ref[0:nheads * nope, :] = wukt_ref[...]

    lhs_ref[nheads * nope:, :] = qt_ref[...]
    nxt = jnp.where(b + 1 < nb, b + 1, 0)

    wait_chunk(0)
    cbs, scores = chunk_scores(0)
    cnb = cn_ref[...].astype(BF16)
    snew = _paged_scores(lhs_ref, qr_ref, cnb, krn_ref[...].astype(BF16), nheads, nq, nope)
    row = lax.broadcasted_iota(jnp.int32, snew.shape, 0)
    col = lax.broadcasted_iota(jnp.int32, snew.shape, 1)
    snew = jnp.where(col * nheads <= row, snew, NEG)
    start_chunk(nxt, 0, 0)

    wait_chunk(1)
    cbs1, scores1 = chunk_scores(1)
    start_chunk(nxt, 1, 1)
    m0, l0, a0 = softmax_block([snew] + scores, [cnb] + cbs)
    m1, l1, a1 = softmax_block(scores1, cbs1)

    m = jnp.maximum(m0, m1)
    w0 = jnp.exp2(m0 - m)
    w1 = jnp.exp2(m1 - m)
    o_ref[...] = (w0 * a0 + w1 * a1) / (w0 * l0 + w1 * l1)

    @pl.when(b == nb - 1)
    def _():
        wait_chunk(0)
        wait_chunk(1)


def _paged_call(page_table, cache_lat, cache_krt, wukt, qt, qr, cn, krn, nheads, nq, nope):
    db, npages = page_table.shape
    _, page, clat = cache_lat.shape
    rdim = cache_krt.shape[1]
    assert npages % 2 == 0
    ch = npages // 2
    sub = min(PAGED_SUB_KEYS, ch * page)
    assert (ch * page) % sub == 0
    nrow = nq * nheads
    per_b = lambda shape: pl.BlockSpec((None,) + shape, lambda b, pt: (b, 0, 0))
    in_specs = [pl.BlockSpec(memory_space=pl.ANY), pl.BlockSpec(memory_space=pl.ANY),
                pl.BlockSpec(wukt.shape, lambda b, pt: (0, 0)),
                per_b((nrow, clat)), per_b((nrow, rdim)), per_b(cn.shape[1:]), per_b(krn.shape[1:])]
    return pl.pallas_call(
        functools.partial(_paged_kernel, npages=npages, ch=ch, sub=sub, nheads=nheads, nq=nq, nope=nope),
        grid_spec=pltpu.PrefetchScalarGridSpec(
            num_scalar_prefetch=1,
            grid=(db,),
            in_specs=in_specs,
            out_specs=per_b((nrow, clat)),
            scratch_shapes=[pltpu.VMEM((2, ch * page, clat), F32), pltpu.VMEM((2, rdim, ch * page), F32),
                            pltpu.SemaphoreType.DMA((2, 2)), pltpu.VMEM((wukt.shape[0] + nrow, clat), BF16)]),
        out_shape=jax.ShapeDtypeStruct((db, nrow, clat), F32),
        compiler_params=_cparams(("arbitrary",)),
        name="mla_paged_attn",
    )(page_table.reshape(-1), cache_lat, cache_krt, wukt, qt, qr, cn, krn)


def _slot(x, lo, width=HEAD_SLOT):
    n = x.shape[-1]
    pad = [(0, 0)] * (x.ndim - 1) + [(lo, width - lo - n)]
    return jnp.pad(x, pad)


def _half_rot_cols(w):
    half = w.shape[-1] // 2
    return jnp.concatenate([-w[..., half:], w[..., :half]], axis=-1)


def _rope_tables(pos, rope, nope):
    half = rope // 2
    inv = ROPE_THETA ** (-jnp.arange(half, dtype=F32) / half)
    ang = pos.astype(F32)[:, None] * inv[None, :]
    cos, sin = jnp.cos(ang), jnp.sin(ang)
    cs = jnp.concatenate([jnp.ones((pos.shape[0], nope), F32), cos, cos], axis=1)
    sn = jnp.concatenate([jnp.zeros((pos.shape[0], nope), F32), sin, sin], axis=1)
    return _slot(cs, 0), _slot(sn, 0)


def kernel(x_prompt, x_sample, state_conv_a, state_ffn_conv, cache_kv_latent, cache_k_rope, page_table, meta_tokens, norm_mix, norm_ffn, a_w_pw1, a_b_pw1, a_w_dw, a_b_dw, a_ln_g, a_ln_b, a_w_pw2, a_b_pw2, ffn_w_up, ffn_w_dw, ffn_w_down, kv_norm, mla_w_dkv, mla_lat_norm, mla_w_kr, mla_knorm_rope, mla_w_uk, mla_w_uv, mla_knorm_nope, mla_w_dq, mla_q_lat_norm, mla_w_uq, mla_qnorm_nope, mla_qnorm_rope, mla_w_o):
    bsz, seq, d = x_prompt.shape
    db, nq, _ = x_sample.shape
    nmeta = meta_tokens.shape[0]
    depth = norm_mix.shape[0]
    n_a = a_w_pw1.shape[0]
    n_b = mla_w_dq.shape[0]
    f = ffn_w_down.shape[1]
    clat, nheads, nope = mla_w_uk.shape
    rope = mla_w_kr.shape[1]
    vdim = mla_w_uv.shape[2]
    width_a = a_w_dw.shape[1]
    npages, page = page_table.shape[1], cache_kv_latent.shape[1]
    past_len = npages * page
    assert n_b == 1 and depth == n_a + n_b and clat == LANES and nope + rope <= HEAD_SLOT
    assert seq >= width_a - 1 and nmeta % 16 == 0 and nmeta >= 8
    tm = ROW_TILE if seq % ROW_TILE == 0 else seq
    rs = nq * db
    row2 = lambda v: v.reshape(1, -1)

    hp = x_prompt
    hm = meta_tokens.astype(F32)[None]
    hs = jnp.transpose(x_sample, (1, 0, 2)).reshape(1, rs, d)

    conv_a_p, conv_a_s, ffn_p, ffn_s = [], [], [], []

    def run_ffn(layer, hp, hm, hs):
        g = row2(norm_ffn[layer])
        wup = ffn_w_up[layer].astype(BF16)
        wdw = jnp.transpose(ffn_w_dw[layer].reshape(3, 2, f), (1, 0, 2))
        wdn = ffn_w_down[layer].astype(BF16)
        st = jnp.transpose(state_ffn_conv[layer].reshape(db, 2, 2, f), (2, 1, 0, 3))
        hm_new, _ = _ffn_call(hm, g, wup, wdw, wdn, nmeta)
        hp_new, tail_p = _ffn_call(hp, g, wup, wdw, wdn, tm, prev=hm[0, nmeta - 8:])
        hs_new, tail_s = _ffn_call(hs, g, wup, wdw, wdn, rs, state=st)
        ffn_p.append(jnp.transpose(tail_p[:, :, 6:8, :], (0, 2, 1, 3)).reshape(bsz, 2, 2 * f))
        ffn_s.append(jnp.transpose(tail_s[0].reshape(2, 2, db, f), (2, 1, 0, 3)).reshape(db, 2, 2 * f))
        return hp_new, hm_new, hs_new

    for layer in range(n_a):
        g = row2(norm_mix[layer])
        w1 = a_w_pw1[layer].astype(BF16)
        b1 = row2(a_b_pw1[layer])
        conv_w = (a_w_dw[layer], row2(a_b_dw[layer]), row2(a_ln_g[layer]), row2(a_ln_b[layer]),
                  a_w_pw2[layer].astype(BF16), row2(a_b_pw2[layer]))
        um = _a1_call(hm, g, w1, b1, nmeta)
        up = _a1_call(hp, g, w1, b1, tm)
        us = _a1_call(hs, g, w1, b1, rs)
        halo = jnp.pad(um[0], ((32 - nmeta, 0), (0, 0))) if nmeta < 32 else um[0, nmeta - 32:]
        hm_new = _a2_seq_call(um, hm, None, *conv_w, nmeta)
        hp = _a2_seq_call(up, hp, halo, *conv_w, tm)
        hm = hm_new
        xp_s = jnp.concatenate([jnp.transpose(state_conv_a[layer], (1, 0, 2)), us.reshape(nq, db, d)], axis=0)
        hs = _a2_tm_call(xp_s, hs.reshape(nq, db, d), *conv_w, 32 if db % 32 == 0 else db).reshape(1, rs, d)
        conv_a_p.append(up[:, seq - (width_a - 1):])
        conv_a_s.append(jnp.transpose(xp_s[nq:], (1, 0, 2)))
        hp, hm, hs = run_ffn(layer, hp, hm, hs)

    layer = n_a
    g_kr = mla_knorm_rope
    wkr_g = mla_w_kr * g_kr[None, :]
    uq = mla_w_uq[0]
    uq_rope_g = uq[:, :, nope:] * mla_qnorm_rope[0][None, None, :]
    pw = {
        "gkv": row2(kv_norm), "gmix": row2(norm_mix[layer]),
        "wkv": jnp.concatenate([mla_w_dkv, _slot(mla_w_kr, nope), _slot(_half_rot_cols(wkr_g), nope)], axis=1).astype(BF16),
        "glat": row2(mla_lat_norm), "gkr": _slot(row2(g_kr), nope),
        "wdq": mla_w_dq[0].astype(BF16), "gql": row2(mla_q_lat_norm[0]),
        "wqm": _slot(uq, 0).reshape(uq.shape[0], nheads * HEAD_SLOT).astype(BF16),
        "wqr": _slot(_half_rot_cols(uq_rope_g), nope).reshape(uq.shape[0], nheads * HEAD_SLOT).astype(BF16),
        "gq": _slot(row2(jnp.concatenate([mla_qnorm_nope[0], mla_qnorm_rope[0]])), 0),
        "wuk": _slot(mla_w_uk, 0).reshape(clat, nheads * HEAD_SLOT).astype(BF16),
        "gk": _slot(row2(mla_knorm_nope), 0),
        "wuv": _slot(mla_w_uv, 0).reshape(clat, nheads * HEAD_SLOT).astype(BF16),
        "vone": jnp.tile(_slot(jnp.ones((1, HEAD_SLOT - vdim), F32), vdim), (1, nheads)),
    }
    cs_m, sn_m = _rope_tables(jnp.arange(nmeta), rope, nope)
    cs_p, sn_p = _rope_tables(nmeta + jnp.arange(seq), rope, nope)
    cs_s, sn_s = _rope_tables(jnp.repeat(past_len + jnp.arange(nq), db), rope, nope)
    proj = functools.partial(_proj_call, w=pw, nheads=nheads, nope=nope, rope=rope)
    c_m, kr_m, q_m, k_m, v_m = proj(hm, cs_m, sn_m, tm=nmeta)
    c_p, kr_p, q_p, k_p, v_p = proj(hp, cs_p, sn_p, tm=tm)
    c_s, kr_s, q_s, _, _ = proj(hs, cs_s, sn_s, tm=rs)
    kr_m, kr_p, kr_s = (x[..., nope:nope + rope] for x in (kr_m, kr_p, kr_s))

    tq = ATTN_Q_TILE if seq % ATTN_Q_TILE == 0 else seq
    tk = ATTN_K_TILE if tq % ATTN_K_TILE == 0 else tq
    dk = ATTN_DIAG_TILE if tq % ATTN_DIAG_TILE == 0 else tq
    o_m = _attn_call(q_m, k_m, v_m, None, None, nmeta, nmeta, nmeta, nheads, vdim)
    o_p = _attn_call(q_p, k_p, v_p, k_m[0], v_m[0], tq, tk, dk, nheads, vdim)

    wukt = jnp.transpose(mla_w_uk.reshape(clat, nheads * nope)).astype(BF16)
    wqt = jnp.transpose(mla_w_uk, (1, 2, 0)) * mla_knorm_nope[None, :, None]
    wqt = jnp.pad(wqt, ((0, 0), (0, HEAD_SLOT - nope), (0, 0))).astype(BF16)
    qt = _qt_call(q_s[0], wqt, nheads)
    qt = jnp.transpose(qt.reshape(nheads, nq, db, clat), (2, 1, 0, 3)).reshape(db, nq * nheads, clat)
    qr = jnp.transpose(q_s[0].reshape(nq, db, nheads, HEAD_SLOT)[..., nope:nope + rope], (1, 0, 2, 3))
    qr = qr.reshape(db, nq * nheads, rope)
    nnew = LANES
    cn = jnp.pad(jnp.transpose(c_s[0].reshape(nq, db, clat), (1, 0, 2)), ((0, 0), (0, nnew - nq), (0, 0)))
    krn = jnp.pad(jnp.transpose(kr_s[0].reshape(nq, db, rope), (1, 2, 0)), ((0, 0), (0, 0), (0, nnew - nq)))
    cache_krt = jnp.transpose(cache_k_rope, (0, 2, 1))
    o_lat = _paged_call(page_table, cache_kv_latent, cache_krt, wukt, qt, qr, cn, krn, nheads, nq, nope)
    ol = jnp.transpose(o_lat.reshape(db, nq, nheads, clat), (2, 1, 0, 3)).reshape(nheads, rs, clat)
    wuv_pad = jnp.stack([_slot(mla_w_uv[:, hd, :], (hd % 2) * vdim, LANES) for hd in range(nheads)]).astype(BF16)
    o_s = _uv_call(ol, wuv_pad, nheads, vdim)[None]

    wo = mla_w_o[0].astype(BF16)
    hm = _wo_call(hm, o_m, wo, nmeta)
    hp = _wo_call(hp, o_p, wo, tm)
    hs = _wo_call(hs, o_s, wo, rs)
    hp, hm, hs = run_ffn(layer, hp, hm, hs)

    y_sample = jnp.transpose(hs.reshape(nq, db, d), (1, 0, 2))
    kv_lat_p = jnp.concatenate([jnp.broadcast_to(c_m, (bsz, nmeta, clat)), c_p], axis=1)
    k_rope_p = jnp.concatenate([jnp.broadcast_to(kr_m, (bsz, nmeta, rope)), kr_p], axis=1)
    kv_lat_s = jnp.transpose(c_s[0].reshape(nq, db, clat), (1, 0, 2))
    k_rope_s = jnp.transpose(kr_s[0].reshape(nq, db, rope), (1, 0, 2))
    return (hp, y_sample, jnp.stack(conv_a_p), jnp.stack(conv_a_s), jnp.stack(ffn_p), jnp.stack(ffn_s),
            kv_lat_p, k_rope_p, kv_lat_s, k_rope_s)
```

```python
import functools
import math

import jax
import jax.numpy as jnp
from jax import lax
from jax.experimental import pallas as pl
from jax.experimental.pallas import tpu as pltpu

F32 = jnp.float32
BF16 = jnp.bfloat16
NORM_EPS = 1e-6
ROPE_THETA = 10000.0
NEG = -1e30

LANES = 128
HEAD_SLOT = 128
ROW_TILE = 512
ATTN_Q_TILE = 1024
ATTN_K_TILE = 1024
ATTN_DIAG_TILE = 512
FFN_CHUNK = 256
FFN_SEQ_CHUNK = 2816
ATTN_HEADS_PER_STEP = 4
PAGED_SUB_KEYS = 2048
LOG2E = 1.4426950408889634
VMEM_LIMIT = 56 * 1024 * 1024


def _cparams(sem):
    return pltpu.CompilerParams(dimension_semantics=sem, vmem_limit_bytes=VMEM_LIMIT)


def _const(shape):
    zeros = (0,) * len(shape)
    return pl.BlockSpec(shape, lambda *_: zeros)


def _sigmoid(x):
    return 1.0 / (1.0 + jnp.exp(-x))


def _rms_scale(x):
    return lax.rsqrt(jnp.mean(x * x, axis=-1, keepdims=True) + NORM_EPS)


def _dot(a, b):
    return jnp.dot(a, b, preferred_element_type=F32)


def _dot_nt(a, b):
    return lax.dot_general(a, b, (((1,), (1,)), ((), ())), preferred_element_type=F32)


def _a1_kernel(x_ref, g_ref, w_ref, b_ref, u_ref):
    x = x_ref[...]
    xn = x * _rms_scale(x) * g_ref[...]
    z = _dot(xn.astype(BF16), w_ref[...]) + b_ref[...]
    d = u_ref.shape[-1]
    u_ref[...] = z[:, :d] * _sigmoid(z[:, d:])


def _a1_call(x, g, w, b, tm):
    bsz, t, d = x.shape
    return pl.pallas_call(
        _a1_kernel,
        grid=(bsz, t // tm),
        in_specs=[pl.BlockSpec((None, tm, d), lambda bi, i: (bi, i, 0)),
                  _const((1, d)), _const((d, 2 * d)), _const((1, 2 * d))],
        out_specs=pl.BlockSpec((None, tm, d), lambda bi, i: (bi, i, 0)),
        out_shape=jax.ShapeDtypeStruct((bsz, t, d), F32),
        compiler_params=_cparams(("parallel", "parallel")),
        name="conv_pw1_glu",
    )(x, g, w, b)


def _ln_swish_pw2(y, h, lng_ref, lnb_ref, w2_ref, b2_ref):
    mu = jnp.mean(y, axis=-1, keepdims=True)
    yc = y - mu
    var = jnp.mean(yc * yc, axis=-1, keepdims=True)
    yn = yc * lax.rsqrt(var + NORM_EPS) * lng_ref[...] + lnb_ref[...]
    a = yn * _sigmoid(yn)
    return h + _dot(a.astype(BF16), w2_ref[...]) + b2_ref[...]


def _a2_seq_kernel(*refs, tm, halo, width, has_prev):
    if has_prev:
        (u_ref, uh_ref, prev_ref, h_ref, wdw_ref, bdw_ref, lng_ref, lnb_ref, w2_ref, b2_ref,
         o_ref, ext_ref, y_ref) = refs
        i = pl.program_id(1)

        @pl.when(i == 0)
        def _():
            ext_ref[0:halo, :] = prev_ref[...]

        @pl.when(i > 0)
        def _():
            ext_ref[0:halo, :] = uh_ref[...]
    else:
        (u_ref, h_ref, wdw_ref, bdw_ref, lng_ref, lnb_ref, w2_ref, b2_ref,
         o_ref, ext_ref, y_ref) = refs
        ext_ref[0:halo, :] = jnp.zeros((halo, ext_ref.shape[1]), F32)
    ext_ref[halo:halo + tm, :] = u_ref[...]
    d = u_ref.shape[-1]
    off = halo - (width - 1)

    taps = {}
    for k in range(width):
        taps.setdefault((off + k) % 8, []).append(((off + k) // 8, k))
    nblk = tm // 8

    def col_body(c, carry):
        c0 = pl.multiple_of(c * LANES, LANES)
        cols = pl.ds(c0, LANES)
        bias = jnp.broadcast_to(bdw_ref[:, cols], (8, LANES))
        sub = lax.broadcasted_iota(jnp.int32, (8, LANES), 0)
        xs = {}

        def xblk(jb):
            if jb not in xs:
                xs[jb] = ext_ref[pl.ds(jb * 8, 8), cols]
            return xs[jb]

        def zsum(r, jb):
            acc = None
            for a, k in taps[r]:
                term = wdw_ref[pl.ds(k * 8, 8), cols] * xblk(jb + a)
                acc = term if acc is None else acc + term
            return acc

        zprev = {r: zsum(r, 0) for r in taps if r != 0}
        for jb in range(nblk):
            y = zsum(0, jb) + bias if 0 in taps else bias
            znext = {}
            for r in zprev:
                znext[r] = zsum(r, jb + 1)
                y = y + pltpu.roll(jnp.where(sub >= r, zprev[r], znext[r]), 8 - r, axis=0)
            y_ref[pl.ds(jb * 8, 8), cols] = y
            zprev = znext
        return carry

    lax.fori_loop(0, d // LANES, col_body, 0)
    o_ref[...] = _ln_swish_pw2(y_ref[...], h_ref[...], lng_ref, lnb_ref, w2_ref, b2_ref)


def _a2_seq_call(u, h, prev, wdw, bdw, lng, lnb, w2, b2, tm):
    bsz, t, d = u.shape
    width = wdw.shape[0]
    halo = 32
    assert halo >= width - 1 and halo % 8 == 0 and (tm % halo == 0 or prev is None)
    wdw = jnp.repeat(wdw, 8, axis=0)
    row = pl.BlockSpec((None, tm, d), lambda bi, i: (bi, i, 0))
    consts = [_const((width * 8, d)), _const((1, d)), _const((1, d)), _const((1, d)), _const((d, d)), _const((1, d))]
    if prev is not None:
        k = tm // halo
        in_specs = [row, pl.BlockSpec((None, halo, d), lambda bi, i: (bi, jnp.maximum(i * k - 1, 0), 0)),
                    _const((halo, d)), row] + consts
        args = (u, u, prev, h, wdw, bdw, lng, lnb, w2, b2)
    else:
        assert t == tm
        in_specs = [row, row] + consts
        args = (u, h, wdw, bdw, lng, lnb, w2, b2)
    return pl.pallas_call(
        functools.partial(_a2_seq_kernel, tm=tm, halo=halo, width=width, has_prev=prev is not None),
        grid=(bsz, t // tm),
        in_specs=in_specs,
        out_specs=row,
        out_shape=jax.ShapeDtypeStruct((bsz, t, d), F32),
        scratch_shapes=[pltpu.VMEM((halo + tm, d), F32), pltpu.VMEM((tm, d), F32)],
        compiler_params=_cparams(("parallel", "parallel")),
        name="conv_dw_ln_pw2",
    )(*args)


def _a2_tm_kernel(xp_ref, h_ref, wdw_ref, bdw_ref, lng_ref, lnb_ref, w2_ref, b2_ref, o_ref, y_ref, *, width, cw):
    nt, bs, d = h_ref.shape
    for c in range(d // cw):
        cols = slice(c * cw, (c + 1) * cw)
        wk = [jnp.broadcast_to(wdw_ref[k:k + 1, cols], (bs, cw)) for k in range(width)]
        bias = bdw_ref[:, cols]
        for t in range(nt):
            acc = wk[0] * xp_ref[t, :, cols]
            for k in range(1, width):
                acc = acc + wk[k] * xp_ref[t + k, :, cols]
            y_ref[t * bs:(t + 1) * bs, cols] = acc + bias
    h = h_ref[...].reshape(nt * bs, d)
    out = _ln_swish_pw2(y_ref[...], h, lng_ref, lnb_ref, w2_ref, b2_ref)
    o_ref[...] = out.reshape(nt, bs, d)


def _a2_tm_call(xp, h, wdw, bdw, lng, lnb, w2, b2, bs):
    nt, db, d = h.shape
    width = wdw.shape[0]
    return pl.pallas_call(
        functools.partial(_a2_tm_kernel, width=width, cw=2 * LANES),
        grid=(db // bs,),
        in_specs=[pl.BlockSpec((nt + width - 1, bs, d), lambda i: (0, i, 0)),
                  pl.BlockSpec((nt, bs, d), lambda i: (0, i, 0)),
                  _const((width, d)), _const((1, d)), _const((1, d)), _const((1, d)), _const((d, d)), _const((1, d))],
        out_specs=pl.BlockSpec((nt, bs, d), lambda i: (0, i, 0)),
        out_shape=jax.ShapeDtypeStruct((nt, db, d), F32),
        scratch_shapes=[pltpu.VMEM((nt * bs, d), F32)],
        compiler_params=_cparams(("parallel",)),
        name="conv_dw_ln_pw2_sample",
    )(xp, h, wdw, bdw, lng, lnb, w2, b2)


def _ffn_kernel(*refs, mode, tm, fc, shift, ngroups, attn):
    if mode == "prev" and attn:
        x_ref, xh_ref, prev_ref, a_ref, ah_ref, wo_ref, g_ref, wg_ref, wu_ref, wdw_ref, wdn_ref, o_ref, tail_ref = refs
    elif mode == "prev":
        x_ref, xh_ref, prev_ref, g_ref, wg_ref, wu_ref, wdw_ref, wdn_ref, o_ref, tail_ref = refs
    elif mode == "zero":
        x_ref, g_ref, wg_ref, wu_ref, wdw_ref, wdn_ref, o_ref, tail_ref = refs
    else:
        x_ref, st_ref, g_ref, wg_ref, wu_ref, wdw_ref, wdn_ref, o_ref, tail_ref = refs
    x = x_ref[...]
    d = x.shape[-1]
    f = wdn_ref.shape[0]
    if mode == "prev":
        first = pl.program_id(1) == 0
        xh = xh_ref[...]
        if attn:
            x = x + _dot(a_ref[...], wo_ref[...])
            xh = xh + _dot(ah_ref[...], wo_ref[...])[8:16]
        halo = jnp.where(first, prev_ref[...], xh)
        xe = jnp.concatenate([halo, x], axis=0)
    elif mode == "zero":
        xe = jnp.concatenate([jnp.zeros((8, d), F32), x], axis=0)
    else:
        xe = x
    pre = xe.shape[0] - tm if mode != "state" else 2 * shift
    xn = (xe * _rms_scale(xe) * g_ref[...]).astype(BF16)
    ntail = tail_ref.shape[1]
    fg = wdn_ref.shape[0]
    acc = jnp.zeros((tm, d), F32)
    for j in range(fg // fc):
        cols = slice(j * fc, (j + 1) * fc)
        act = None
        for gu in range(2):
            hu = _dot(xn, (wg_ref, wu_ref)[gu][:, cols])
            tail_ref[gu, :, cols] = hu[hu.shape[0] - ntail:, :]
            if mode == "state":
                hu = jnp.concatenate([st_ref[gu, 0, :, cols], st_ref[gu, 1, :, cols], hu], axis=0)
            w = wdw_ref[gu, :, cols]
            conv = (w[0:1] * hu[pre - 2 * shift:pre - 2 * shift + tm]
                    + w[1:2] * hu[pre - shift:pre - shift + tm]
                    + w[2:3] * hu[pre:pre + tm])
            act = conv * _sigmoid(conv) if gu == 0 else act * conv
        acc = acc + _dot(act.astype(BF16), wdn_ref[cols, :])
    if ngroups == 1:
        o_ref[...] = x + acc
    else:
        gj = pl.program_id(2)

        @pl.when(gj == 0)
        def _():
            o_ref[...] = x + acc

        @pl.when(gj > 0)
        def _():
            o_ref[...] = o_ref[...] + acc


def _ffn_call(x, g, wup, wdw, wdn, tm, prev=None, state=None, attn=None):
    bsz, t, d = x.shape
    f = wdn.shape[0]
    fc = FFN_CHUNK if f % FFN_CHUNK == 0 else LANES
    if state is None and f % FFN_SEQ_CHUNK == 0:
        fc = FFN_SEQ_CHUNK
    fg = fc if state is not None else f
    ngroups = f // fg
    row = pl.BlockSpec((None, tm, d), lambda bi, i, gj: (bi, i, 0))
    consts = [pl.BlockSpec((1, d), lambda bi, i, gj: (0, 0)),
              pl.BlockSpec((d, fg), lambda bi, i, gj: (0, gj)),
              pl.BlockSpec((d, fg), lambda bi, i, gj: (0, ngroups + gj)),
              pl.BlockSpec((2, 3, fg), lambda bi, i, gj: (0, 0, gj)),
              pl.BlockSpec((fg, d), lambda bi, i, gj: (gj, 0))]
    if state is not None:
        assert bsz == 1 and t == tm
        mode, shift, ntail = "state", state.shape[2], 2 * state.shape[2]
        in_specs = [row, pl.BlockSpec(state.shape[:3] + (fg,), lambda bi, i, gj: (0, 0, 0, gj))] + consts
        args = (x, state, g, wup, wup, wdw, wdn)
    elif prev is not None:
        mode, shift, ntail = "prev", 1, 8
        k = tm // 8
        in_specs = [row, pl.BlockSpec((None, 8, d), lambda bi, i, gj: (bi, jnp.maximum(i * k - 1, 0), 0)),
                    pl.BlockSpec((8, d), lambda bi, i, gj: (0, 0))]
        args = (x, x, prev)
        if attn is not None:
            a, wo = attn
            hv = a.shape[-1]
            k16 = tm // 16
            in_specs += [pl.BlockSpec((None, tm, hv), lambda bi, i, gj: (bi, i, 0)),
                         pl.BlockSpec((None, 16, hv), lambda bi, i, gj: (bi, jnp.maximum(i * k16 - 1, 0), 0)),
                         pl.BlockSpec((hv, d), lambda bi, i, gj: (0, 0))]
            args += (a, a, wo)
        in_specs += consts
        args += (g, wup, wup, wdw, wdn)
    else:
        assert t == tm
        mode, shift, ntail = "zero", 1, 8
        in_specs = [row] + consts
        args = (x, g, wup, wup, wdw, wdn)
    out, tail = pl.pallas_call(
        functools.partial(_ffn_kernel, mode=mode, tm=tm, fc=fc, shift=shift, ngroups=ngroups,
                          attn=attn is not None),
        grid=(bsz, t // tm, ngroups),
        in_specs=in_specs,
        out_specs=[row, pl.BlockSpec((None, 2, ntail, fg), lambda bi, i, gj: (bi, 0, 0, gj))],
        out_shape=[jax.ShapeDtypeStruct((bsz, t, d), F32), jax.ShapeDtypeStruct((bsz, 2, ntail, f), F32)],
        compiler_params=_cparams(("parallel", "arbitrary", "arbitrary")),
        name="conv_ffn_" + mode,
    )(*args)
    return out, tail


def _wo_kernel(h_ref, o_ref, w_ref, out_ref):
    out_ref[...] = h_ref[...] + _dot(o_ref[...], w_ref[...])


def _wo_call(h, o, w, tm):
    bsz, t, d = h.shape
    hv = o.shape[-1]
    return pl.pallas_call(
        _wo_kernel,
        grid=(bsz, t // tm),
        in_specs=[pl.BlockSpec((None, tm, d), lambda bi, i: (bi, i, 0)),
                  pl.BlockSpec((None, tm, hv), lambda bi, i: (bi, i, 0)), _const((hv, d))],
        out_specs=pl.BlockSpec((None, tm, d), lambda bi, i: (bi, i, 0)),
        out_shape=jax.ShapeDtypeStruct((bsz, t, d), F32),
        compiler_params=_cparams(("parallel", "parallel")),
        name="attn_out_proj",
    )(h, o, w)


def _proj_kernel(h_ref, cs_ref, sn_ref, gkv_ref, gmix_ref, wkv_ref, glat_ref, gkr_ref, wdq_ref, gql_ref,
                 wqm_ref, wqr_ref, gq_ref, wuk_ref, gk_ref, wuv_ref, vone_ref,
                 c_ref, kr_ref, q_ref, k_ref, v_ref, *, nheads, nope, rope, scale):
    h = h_ref[...]
    hs = h * _rms_scale(h)
    hn = (hs * gkv_ref[...]).astype(BF16)
    xq = (hs * gmix_ref[...]).astype(BF16)
    z = _dot(hn, wkv_ref[...])
    cpre, a, ar = z[:, :LANES], z[:, LANES:2 * LANES], z[:, 2 * LANES:3 * LANES]
    c = cpre * _rms_scale(cpre) * glat_ref[...]
    c_ref[...] = c
    cs = cs_ref[...]
    sn = sn_ref[...]
    ra = lax.rsqrt(jnp.sum(a * a, axis=-1, keepdims=True) * (1.0 / rope) + NORM_EPS)
    kr = ra * (a * gkr_ref[...] * cs + ar * sn)
    kr_ref[...] = kr
    cb = c.astype(BF16)
    v_ref[...] = (_dot(cb, wuv_ref[...]) + vone_ref[...]).astype(BF16)
    kn = _dot(cb, wuk_ref[...])
    ql = _dot(xq, wdq_ref[...])
    ql = (ql * _rms_scale(ql) * gql_ref[...]).astype(BF16)
    qm = _dot(ql, wqm_ref[...])
    qr = _dot(ql, wqr_ref[...])
    lane = lax.broadcasted_iota(jnp.int32, (1, HEAD_SLOT), 1)
    mn = (lane < nope).astype(F32)
    mr = jnp.logical_and(lane >= nope, lane < nope + rope).astype(F32)
    csq = cs * (gq_ref[...] * scale)
    snq = sn * scale
    gk = gk_ref[...]
    for hd in range(nheads):
        sl = slice(hd * HEAD_SLOT, (hd + 1) * HEAD_SLOT)
        qh = qm[:, sl]
        sq = qh * qh
        rn = lax.rsqrt(jnp.sum(sq * mn, axis=-1, keepdims=True) * (1.0 / nope) + NORM_EPS)
        rr = lax.rsqrt(jnp.sum(sq * mr, axis=-1, keepdims=True) * (1.0 / rope) + NORM_EPS)
        qv = (qh * csq + qr[:, sl] * snq) * jnp.where(lane < nope, rn, rr)
        q_ref[:, sl] = qv.astype(BF16)
        kh = kn[:, sl]
        rk = lax.rsqrt(jnp.sum(kh * kh, axis=-1, keepdims=True) * (1.0 / nope) + NORM_EPS)
        k_ref[:, sl] = (kh * rk * gk + kr).astype(BF16)


def _proj_call(h, cs, sn, w, tm, nheads, nope, rope):
    bsz, t, d = h.shape
    ql = w["wdq"].shape[1]
    hs = nheads * HEAD_SLOT
    row = lambda n: pl.BlockSpec((None, tm, n), lambda bi, i: (bi, i, 0))
    tab = pl.BlockSpec((tm, HEAD_SLOT), lambda bi, i: (i, 0))
    in_specs = [row(d), tab, tab, _const((1, d)), _const((1, d)), _const((d, 3 * LANES)), _const((1, LANES)),
                _const((1, HEAD_SLOT)), _const((d, ql)), _const((1, ql)), _const((ql, hs)), _const((ql, hs)),
                _const((1, HEAD_SLOT)), _const((LANES, hs)), _const((1, HEAD_SLOT)), _const((LANES, hs)),
                _const((1, hs))]
    return pl.pallas_call(
        functools.partial(_proj_kernel, nheads=nheads, nope=nope, rope=rope,
                          scale=LOG2E / math.sqrt(nope + rope)),
        grid=(bsz, t // tm),
        in_specs=in_specs,
        out_specs=[row(LANES), row(HEAD_SLOT), row(hs), row(hs), row(hs)],
        out_shape=[jax.ShapeDtypeStruct((bsz, t, LANES), F32), jax.ShapeDtypeStruct((bsz, t, HEAD_SLOT), F32),
                   jax.ShapeDtypeStruct((bsz, t, hs), BF16), jax.ShapeDtypeStruct((bsz, t, hs), BF16),
                   jax.ShapeDtypeStruct((bsz, t, hs), BF16)],
        compiler_params=_cparams(("parallel", "parallel")),
        name="mla_proj",
    )(h, cs, sn, w["gkv"], w["gmix"], w["wkv"], w["glat"], w["gkr"], w["wdq"], w["gql"], w["wqm"], w["wqr"],
      w["gq"], w["wuk"], w["gk"], w["wuv"], w["vone"])


def _attn_update(hd, r0, nr, q_ref, kb, vb, mask, m_ref, acc_ref, first):
    rows = slice(r0, r0 + nr)
    q = q_ref[rows, hd * HEAD_SLOT:(hd + 1) * HEAD_SLOT]
    s = _dot_nt(q, kb)
    if mask is not None:
        s = jnp.where(mask, s, NEG)
    nk = s.shape[1]
    m_cur = jnp.max(s, axis=1, keepdims=True)
    if first:
        m_new = jnp.broadcast_to(m_cur, (nr, LANES))
    else:
        m_prev = m_ref[hd, rows, :]
        m_new = jnp.maximum(m_prev, m_cur)
    if nk % LANES == 0:
        mb = jnp.concatenate([m_new] * (nk // LANES), axis=1) if nk > LANES else m_new
    else:
        mb = m_new[:, :nk]
    p = jnp.exp2((s - mb).astype(BF16))
    pv = _dot(p, vb)
    if first:
        acc_ref[hd, rows, :] = pv
    else:
        alpha = jnp.exp2(m_prev - m_new)
        acc_ref[hd, rows, :] = alpha * acc_ref[hd, rows, :] + pv
    m_ref[hd, rows, :] = m_new


def _attn_kernel(*refs, tq, tk, dk, has_prefix, vdim, hps):
    if has_prefix:
        q_ref, k_ref, v_ref, pk_ref, pv_ref, o_ref, m_ref, acc_ref = refs
    else:
        q_ref, k_ref, v_ref, o_ref, m_ref, acc_ref = refs
    i = pl.program_id(2)
    cols = lambda hd: slice(hd * HEAD_SLOT, (hd + 1) * HEAD_SLOT)
    if has_prefix:
        for hd in range(hps):
            _attn_update(hd, 0, tq, q_ref, pk_ref[:, cols(hd)], pv_ref[:, cols(hd)], None, m_ref, acc_ref, True)
        nfull = i * (tq // tk)

        def body(j, carry):
            k0 = pl.multiple_of(j * tk, tk)
            for hd in range(hps):
                _attn_update(hd, 0, tq, q_ref, k_ref[pl.ds(k0, tk), cols(hd)], v_ref[pl.ds(k0, tk), cols(hd)],
                             None, m_ref, acc_ref, False)
            return carry

        lax.fori_loop(0, nfull, body, 0)
    for jj in range(tq // dk):
        r0 = jj * dk
        nr = tq - r0
        k0 = pl.multiple_of(i * tq + r0, dk)
        row = lax.broadcasted_iota(jnp.int32, (nr, dk), 0)
        col = lax.broadcasted_iota(jnp.int32, (nr, dk), 1)
        mask = col <= row
        for hd in range(hps):
            _attn_update(hd, r0, nr, q_ref, k_ref[pl.ds(k0, dk), cols(hd)], v_ref[pl.ds(k0, dk), cols(hd)], mask,
                         m_ref, acc_ref, (not has_prefix) and jj == 0)
    lane = lax.broadcasted_iota(jnp.int32, (tq, LANES), 1)
    for pr in range(hps // 2):
        a0 = acc_ref[2 * pr]
        a1 = acc_ref[2 * pr + 1]
        o0 = a0 / pltpu.roll(a0, vdim, axis=1)
        o1 = pltpu.roll(a1, vdim, axis=1) / a1
        o_ref[:, pr * LANES:(pr + 1) * LANES] = jnp.where(lane < vdim, o0, o1).astype(o_ref.dtype)


def _attn_call(q, k, v, pk, pv, tq, tk, dk, nheads, vdim):
    bsz, t, _ = q.shape
    hps = ATTN_HEADS_PER_STEP if nheads % ATTN_HEADS_PER_STEP == 0 else 2
    assert 2 * vdim == HEAD_SLOT and nheads % hps == 0 and t % tq == 0 and tq % tk == 0 and tq % dk == 0
    has_prefix = pk is not None
    assert has_prefix or t == tq
    tile = pl.BlockSpec((None, tq, hps * HEAD_SLOT), lambda bi, hg, i: (bi, i, hg))
    seq = pl.BlockSpec((None, t, hps * HEAD_SLOT), lambda bi, hg, i: (bi, 0, hg))
    in_specs = [tile, seq, seq]
    args = [q, k, v]
    if has_prefix:
        pre = pl.BlockSpec((pk.shape[0], hps * HEAD_SLOT), lambda bi, hg, i: (0, hg))
        in_specs += [pre, pre]
        args += [pk, pv]
    return pl.pallas_call(
        functools.partial(_attn_kernel, tq=tq, tk=tk, dk=dk, has_prefix=has_prefix, vdim=vdim, hps=hps),
        grid=(bsz, nheads // hps, t // tq),
        in_specs=in_specs,
        out_specs=pl.BlockSpec((None, tq, hps * vdim), lambda bi, hg, i: (bi, i, hg)),
        out_shape=jax.ShapeDtypeStruct((bsz, t, nheads * vdim), BF16),
        scratch_shapes=[pltpu.VMEM((hps, tq, LANES), F32)] * 2,
        compiler_params=_cparams(("parallel", "parallel", "arbitrary")),
        name="mla_prompt_attn",
    )(*args)


def _qt_kernel(q_ref, w_ref, o_ref, *, nheads):
    for hd in range(nheads):
        o_ref[hd] = _dot(q_ref[:, hd * HEAD_SLOT:(hd + 1) * HEAD_SLOT], w_ref[hd]).astype(BF16)


def _qt_call(q, w, nheads):
    r = q.shape[0]
    return pl.pallas_call(
        functools.partial(_qt_kernel, nheads=nheads),
        out_shape=jax.ShapeDtypeStruct((nheads, r, LANES), BF16),
        compiler_params=pltpu.CompilerParams(vmem_limit_bytes=VMEM_LIMIT),
        name="mla_absorb_q",
    )(q, w)


def _uv_kernel(ol_ref, w_ref, o_ref, *, nheads):
    for hp in range(nheads // 2):
        r = (_dot(ol_ref[2 * hp].astype(BF16), w_ref[2 * hp])
             + _dot(ol_ref[2 * hp + 1].astype(BF16), w_ref[2 * hp + 1]))
        o_ref[:, hp * LANES:(hp + 1) * LANES] = r.astype(BF16)


def _uv_call(ol, w, nheads, vdim):
    r = ol.shape[1]
    return pl.pallas_call(
        functools.partial(_uv_kernel, nheads=nheads),
        out_shape=jax.ShapeDtypeStruct((r, nheads * vdim), BF16),
        compiler_params=pltpu.CompilerParams(vmem_limit_bytes=VMEM_LIMIT),
        name="mla_value_up",
    )(ol, w)


def _paged_scores(lhs_ref, qr_ref, cb, krt, nheads, nq, nope):
    n = cb.shape[0]
    r = _dot_nt(lhs_ref[...], cb)
    kt = r[:nheads * nope]
    sraw = r[nheads * nope:]
    ss = jnp.sum((kt * kt).reshape(nheads, nope, n), axis=1)
    rs = lax.rsqrt(ss * (1.0 / nope) + NORM_EPS)
    srope = _dot(qr_ref[...], krt)
    return (sraw.reshape(nq, nheads, n) * rs[None, :, :]).reshape(nq * nheads, n) + srope


def _paged_kernel(pt_ref, lat_hbm, krt_hbm, wukt_ref, qt_ref, qr_ref, cn_ref, krn_ref, o_ref,
                  latbuf, krbuf, sem, lhs_ref, *, npages, ch, sub, nheads, nq, nope):
    b = pl.program_id(0)
    nb = pl.num_programs(0)
    page = latbuf.shape[1] // ch

    def lat_copy(pid, slot, p):
        return pltpu.make_async_copy(lat_hbm.at[pid], latbuf.at[slot, pl.ds(p * page, page), :], sem.at[slot, 0])

    def krt_copy(pid, slot, p):
        return pltpu.make_async_copy(krt_hbm.at[pid], krbuf.at[slot, :, pl.ds(p * page, page)], sem.at[slot, 1])

    def start_chunk(bb, c, slot):
        for p in range(ch):
            pid = pt_ref[bb * npages + c * ch + p]
            lat_copy(pid, slot, p).start()
            krt_copy(pid, slot, p).start()

    def wait_chunk(slot):
        for p in range(ch):
            lat_copy(0, slot, p).wait()
            krt_copy(0, slot, p).wait()

    def chunk_scores(slot):
        cbs, scores = [], []
        for sb in range(ch * page // sub):
            keys = slice(sb * sub, (sb + 1) * sub)
            cbs.append(latbuf[slot, keys, :].astype(BF16))
            krt = krbuf[slot, :, keys].astype(BF16)
            scores.append(_paged_scores(lhs_ref, qr_ref, cbs[-1], krt, nheads, nq, nope))
        return cbs, scores

    def softmax_block(scores, cbs):
        s = jnp.concatenate(scores, axis=1)
        m = jnp.max(s, axis=1, keepdims=True)
        p = jnp.exp2(s - m)
        return m, jnp.sum(p, axis=1, keepdims=True), _dot(p.astype(BF16), jnp.concatenate(cbs, axis=0))

    @pl.when(b == 0)
    def _():
        start_chunk(0, 0, 0)
        start_chunk(0, 1, 1)
        lhs_ref[0:nheads * nope, :] = wukt_ref[...]

    lhs_ref[nheads * nope:, :] = qt_ref[...]
    nxt = jnp.where(b + 1 < nb, b + 1, 0)

    wait_chunk(0)
    cbs, scores = chunk_scores(0)
    cnb = cn_ref[...].astype(BF16)
    snew = _paged_scores(lhs_ref, qr_ref, cnb, krn_ref[...].astype(BF16), nheads, nq, nope)
    row = lax.broadcasted_iota(jnp.int32, snew.shape, 0)
    col = lax.broadcasted_iota(jnp.int32, snew.shape, 1)
    snew = jnp.where(col * nheads <= row, snew, NEG)
    start_chunk(nxt, 0, 0)

    wait_chunk(1)
    cbs1, scores1 = chunk_scores(1)
    start_chunk(nxt, 1, 1)
    m0, l0, a0 = softmax_block([snew] + scores, [cnb] + cbs)
    m1, l1, a1 = softmax_block(scores1, cbs1)

    m = jnp.maximum(m0, m1)
    w0 = jnp.exp2(m0 - m)
    w1 = jnp.exp2(m1 - m)
    o_ref[...] = (w0 * a0 + w1 * a1) / (w0 * l0 + w1 * l1)

    @pl.when(b == nb - 1)
    def _():
        wait_chunk(0)
        wait_chunk(1)


def _paged_call(page_table, cache_lat, cache_krt, wukt, qt, qr, cn, krn, nheads, nq, nope):
    db, npages = page_table.shape
    _, page, clat = cache_lat.shape
    rdim = cache_krt.shape[1]
    assert npages % 2 == 0
    ch = npages // 2
    sub = min(PAGED_SUB_KEYS, ch * page)
    assert (ch * page) % sub == 0
    nrow = nq * nheads
    per_b = lambda shape: pl.BlockSpec((None,) + shape, lambda b, pt: (b, 0, 0))
    in_specs = [pl.BlockSpec(memory_space=pl.ANY), pl.BlockSpec(memory_space=pl.ANY),
                pl.BlockSpec(wukt.shape, lambda b, pt: (0, 0)),
                per_b((nrow, clat)), per_b((nrow, rdim)), per_b(cn.shape[1:]), per_b(krn.shape[1:])]
    return pl.pallas_call(
        functools.partial(_paged_kernel, npages=npages, ch=ch, sub=sub, nheads=nheads, nq=nq, nope=nope),
        grid_spec=pltpu.PrefetchScalarGridSpec(
            num_scalar_prefetch=1,
            grid=(db,),
            in_specs=in_specs,
            out_specs=per_b((nrow, clat)),
            scratch_shapes=[pltpu.VMEM((2, ch * page, clat), F32), pltpu.VMEM((2, rdim, ch * page), F32),
                            pltpu.SemaphoreType.DMA((2, 2)), pltpu.VMEM((wukt.shape[0] + nrow, clat), BF16)]),
        out_shape=jax.ShapeDtypeStruct((db, nrow, clat), F32),
        compiler_params=_cparams(("arbitrary",)),
        name="mla_paged_attn",
    )(page_table.reshape(-1), cache_lat, cache_krt, wukt, qt, qr, cn, krn)


def _slot(x, lo, width=HEAD_SLOT):
    n = x.shape[-1]
    pad = [(0, 0)] * (x.ndim - 1) + [(lo, width - lo - n)]
    return jnp.pad(x, pad)


def _half_rot_cols(w):
    half = w.shape[-1] // 2
    return jnp.concatenate([-w[..., half:], w[..., :half]], axis=-1)


def _rope_tables(pos, rope, nope):
    half = rope // 2
    inv = ROPE_THETA ** (-jnp.arange(half, dtype=F32) / half)
    ang = pos.astype(F32)[:, None] * inv[None, :]
    cos, sin = jnp.cos(ang), jnp.sin(ang)
    cs = jnp.concatenate([jnp.ones((pos.shape[0], nope), F32), cos, cos], axis=1)
    sn = jnp.concatenate([jnp.zeros((pos.shape[0], nope), F32), sin, sin], axis=1)
    return _slot(cs, 0), _slot(sn, 0)


def kernel(x_prompt, x_sample, state_conv_a, state_ffn_conv, cache_kv_latent, cache_k_rope, page_table, meta_tokens, norm_mix, norm_ffn, a_w_pw1, a_b_pw1, a_w_dw, a_b_dw, a_ln_g, a_ln_b, a_w_pw2, a_b_pw2, ffn_w_up, ffn_w_dw, ffn_w_down, kv_norm, mla_w_dkv, mla_lat_norm, mla_w_kr, mla_knorm_rope, mla_w_uk, mla_w_uv, mla_knorm_nope, mla_w_dq, mla_q_lat_norm, mla_w_uq, mla_qnorm_nope, mla_qnorm_rope, mla_w_o):
    bsz, seq, d = x_prompt.shape
    db, nq, _ = x_sample.shape
    nmeta = meta_tokens.shape[0]
    depth = norm_mix.shape[0]
    n_a = a_w_pw1.shape[0]
    n_b = mla_w_dq.shape[0]
    f = ffn_w_down.shape[1]
    clat, nheads, nope = mla_w_uk.shape
    rope = mla_w_kr.shape[1]
    vdim = mla_w_uv.shape[2]
    width_a = a_w_dw.shape[1]
    npages, page = page_table.shape[1], cache_kv_latent.shape[1]
    past_len = npages * page
    assert n_b == 1 and depth == n_a + n_b and clat == LANES and nope + rope <= HEAD_SLOT
    assert seq >= width_a - 1 and nmeta % 16 == 0 and nmeta >= 8
    tm = ROW_TILE if seq % ROW_TILE == 0 else seq
    rs = nq * db
    row2 = lambda v: v.reshape(1, -1)

    hp = x_prompt
    hm = meta_tokens.astype(F32)[None]
    hs = jnp.transpose(x_sample, (1, 0, 2)).reshape(1, rs, d)

    conv_a_p, conv_a_s, ffn_p, ffn_s = [], [], [], []

    def run_ffn(layer, hp, hm, hs, attn_p=None):
        g = row2(norm_ffn[layer])
        wup = ffn_w_up[layer].astype(BF16)
        wdw = jnp.transpose(ffn_w_dw[layer].reshape(3, 2, f), (1, 0, 2))
        wdn = ffn_w_down[layer].astype(BF16)
        st = jnp.transpose(state_ffn_conv[layer].reshape(db, 2, 2, f), (2, 1, 0, 3))
        hm_new, _ = _ffn_call(hm, g, wup, wdw, wdn, nmeta)
        hp_new, tail_p = _ffn_call(hp, g, wup, wdw, wdn, tm, prev=hm[0, nmeta - 8:], attn=attn_p)
        hs_new, tail_s = _ffn_call(hs, g, wup, wdw, wdn, rs, state=st)
        ffn_p.append(jnp.transpose(tail_p[:, :, 6:8, :], (0, 2, 1, 3)).reshape(bsz, 2, 2 * f))
        ffn_s.append(jnp.transpose(tail_s[0].reshape(2, 2, db, f), (2, 1, 0, 3)).reshape(db, 2, 2 * f))
        return hp_new, hm_new, hs_new

    for layer in range(n_a):
        g = row2(norm_mix[layer])
        w1 = a_w_pw1[layer].astype(BF16)
        b1 = row2(a_b_pw1[layer])
        conv_w = (a_w_dw[layer], row2(a_b_dw[layer]), row2(a_ln_g[layer]), row2(a_ln_b[layer]),
                  a_w_pw2[layer].astype(BF16), row2(a_b_pw2[layer]))
        um = _a1_call(hm, g, w1, b1, nmeta)
        up = _a1_call(hp, g, w1, b1, tm)
        us = _a1_call(hs, g, w1, b1, rs)
        halo = jnp.pad(um[0], ((32 - nmeta, 0), (0, 0))) if nmeta < 32 else um[0, nmeta - 32:]
        hm_new = _a2_seq_call(um, hm, None, *conv_w, nmeta)
        hp = _a2_seq_call(up, hp, halo, *conv_w, tm)
        hm = hm_new
        xp_s = jnp.concatenate([jnp.transpose(state_conv_a[layer], (1, 0, 2)), us.reshape(nq, db, d)], axis=0)
        hs = _a2_tm_call(xp_s, hs.reshape(nq, db, d), *conv_w, 32 if db % 32 == 0 else db).reshape(1, rs, d)
        conv_a_p.append(up[:, seq - (width_a - 1):])
        conv_a_s.append(jnp.transpose(xp_s[nq:], (1, 0, 2)))
        hp, hm, hs = run_ffn(layer, hp, hm, hs)

    layer = n_a
    g_kr = mla_knorm_rope
    wkr_g = mla_w_kr * g_kr[None, :]
    uq = mla_w_uq[0]
    uq_rope_g = uq[:, :, nope:] * mla_qnorm_rope[0][None, None, :]
    pw = {
        "gkv": row2(kv_norm), "gmix": row2(norm_mix[layer]),
        "wkv": jnp.concatenate([mla_w_dkv, _slot(mla_w_kr, nope), _slot(_half_rot_cols(wkr_g), nope)], axis=1).astype(BF16),
        "glat": row2(mla_lat_norm), "gkr": _slot(row2(g_kr), nope),
        "wdq": mla_w_dq[0].astype(BF16), "gql": row2(mla_q_lat_norm[0]),
        "wqm": _slot(uq, 0).reshape(uq.shape[0], nheads * HEAD_SLOT).astype(BF16),
        "wqr": _slot(_half_rot_cols(uq_rope_g), nope).reshape(uq.shape[0], nheads * HEAD_SLOT).astype(BF16),
        "gq": _slot(row2(jnp.concatenate([mla_qnorm_nope[0], mla_qnorm_rope[0]])), 0),
        "wuk": _slot(mla_w_uk, 0).reshape(clat, nheads * HEAD_SLOT).astype(BF16),
        "gk": _slot(row2(mla_knorm_nope), 0),
        "wuv": _slot(mla_w_uv, 0).reshape(clat, nheads * HEAD_SLOT).astype(BF16),
        "vone": jnp.tile(_slot(jnp.ones((1, HEAD_SLOT - vdim), F32), vdim), (1, nheads)),
    }
    cs_m, sn_m = _rope_tables(jnp.arange(nmeta), rope, nope)
    cs_p, sn_p = _rope_tables(nmeta + jnp.arange(seq), rope, nope)
    cs_s, sn_s = _rope_tables(jnp.repeat(past_len + jnp.arange(nq), db), rope, nope)
    proj = functools.partial(_proj_call, w=pw, nheads=nheads, nope=nope, rope=rope)
    c_m, kr_m, q_m, k_m, v_m = proj(hm, cs_m, sn_m, tm=nmeta)
    c_p, kr_p, q_p, k_p, v_p = proj(hp, cs_p, sn_p, tm=tm)
    c_s, kr_s, q_s, _, _ = proj(hs, cs_s, sn_s, tm=rs)
    kr_m, kr_p, kr_s = (x[..., nope:nope + rope] for x in (kr_m, kr_p, kr_s))

    tq = ATTN_Q_TILE if seq % ATTN_Q_TILE == 0 else seq
    tk = ATTN_K_TILE if tq % ATTN_K_TILE == 0 else tq
    dk = ATTN_DIAG_TILE if tq % ATTN_DIAG_TILE == 0 else tq
    o_m = _attn_call(q_m, k_m, v_m, None, None, nmeta, nmeta, nmeta, nheads, vdim)
    o_p = _attn_call(q_p, k_p, v_p, k_m[0], v_m[0], tq, tk, dk, nheads, vdim)

    wukt = jnp.transpose(mla_w_uk.reshape(clat, nheads * nope)).astype(BF16)
    wqt = jnp.transpose(mla_w_uk, (1, 2, 0)) * mla_knorm_nope[None, :, None]
    wqt = jnp.pad(wqt, ((0, 0), (0, HEAD_SLOT - nope), (0, 0))).astype(BF16)
    qt = _qt_call(q_s[0], wqt, nheads)
    qt = jnp.transpose(qt.reshape(nheads, nq, db, clat), (2, 1, 0, 3)).reshape(db, nq * nheads, clat)
    qr = jnp.transpose(q_s[0].reshape(nq, db, nheads, HEAD_SLOT)[..., nope:nope + rope], (1, 0, 2, 3))
    qr = qr.reshape(db, nq * nheads, rope)
    nnew = LANES
    cn = jnp.pad(jnp.transpose(c_s[0].reshape(nq, db, clat), (1, 0, 2)), ((0, 0), (0, nnew - nq), (0, 0)))
    krn = jnp.pad(jnp.transpose(kr_s[0].reshape(nq, db, rope), (1, 2, 0)), ((0, 0), (0, 0), (0, nnew - nq)))
    cache_krt = jnp.transpose(cache_k_rope, (0, 2, 1))
    o_lat = _paged_call(page_table, cache_kv_latent, cache_krt, wukt, qt, qr, cn, krn, nheads, nq, nope)
    ol = jnp.transpose(o_lat.reshape(db, nq, nheads, clat), (2, 1, 0, 3)).reshape(nheads, rs, clat)
    wuv_pad = jnp.stack([_slot(mla_w_uv[:, hd, :], (hd % 2) * vdim, LANES) for hd in range(nheads)]).astype(BF16)
    o_s = _uv_call(ol, wuv_pad, nheads, vdim)[None]

    wo = mla_w_o[0].astype(BF16)
    hm = _wo_call(hm, o_m, wo, nmeta)
    hs = _wo_call(hs, o_s, wo, rs)
    hp, hm, hs = run_ffn(layer, hp, hm, hs, attn_p=(o_p, wo))

    y_sample = jnp.transpose(hs.reshape(nq, db, d), (1, 0, 2))
    kv_lat_p = jnp.concatenate([jnp.broadcast_to(c_m, (bsz, nmeta, clat)), c_p], axis=1)
    k_rope_p = jnp.concatenate([jnp.broadcast_to(kr_m, (bsz, nmeta, rope)), kr_p], axis=1)
    kv_lat_s = jnp.transpose(c_s[0].reshape(nq, db, clat), (1, 0, 2))
    k_rope_s = jnp.transpose(kr_s[0].reshape(nq, db, rope), (1, 0, 2))
    return (hp, y_sample, jnp.stack(conv_a_p), jnp.stack(conv_a_s), jnp.stack(ffn_p), jnp.stack(ffn_s),
            kv_lat_p, k_rope_p, kv_lat_s, k_rope_s)
```
